```python
import math
import jax, jax.numpy as jnp
from jax import lax
import numpy as np

D_MODEL = 1024
BATCH = 2
SEQ = 16384
DEPTH = 2

HEAD_DIM = 64
SWA_Q_HEADS = 8
SWA_KV_HEADS = 2
SWA_GROUP = SWA_Q_HEADS // SWA_KV_HEADS
SWA_WINDOW = 128
MOBA_HEADS = 8
MOBA_BLOCK = 256
MOBA_TOPK = 3
MOBA_Q_CHUNK = 128
HGRN_HEADS = 8
HGRN_DK = D_MODEL // HGRN_HEADS
HGRN_DV = D_MODEL // HGRN_HEADS
HGRN_WIDTH = HGRN_HEADS * HGRN_DK
HGRN_CHUNK = 64
D_FF = 2816
N_EVEN = (DEPTH + 1) // 2
N_ODD = DEPTH // 2
DEEPNORM_ALPHA = (2 * DEPTH) ** 0.25
DEEPNORM_BETA = (8 * DEPTH) ** -0.25
FFN_RES_WEIGHT = 0.5
LN_EPS = 1e-5
RMS_EPS = 1e-6
ATTN_SCALE = HEAD_DIM ** -0.5
SWA_Q_W = SWA_Q_HEADS * HEAD_DIM
SWA_KV_W = SWA_KV_HEADS * HEAD_DIM
MOBA_W = MOBA_HEADS * HEAD_DIM
EVEN_SPLITS = (SWA_Q_W, SWA_Q_W + SWA_KV_W, SWA_Q_W + 2 * SWA_KV_W,
               SWA_Q_W + 2 * SWA_KV_W + MOBA_W, SWA_Q_W + 2 * SWA_KV_W + 2 * MOBA_W)
EVEN_IN_WIDTH = SWA_Q_W + 2 * SWA_KV_W + 3 * MOBA_W
EVEN_OUT_WIDTH = SWA_Q_W + MOBA_W

kernel_name = "hybrid_swa_moba_hgrn2_deepnorm_macaron"


def layer_norm(x, g, b):
    xf = x.astype(jnp.float32)
    mu = jnp.mean(xf, axis=-1, keepdims=True)
    var = jnp.mean(jnp.square(xf - mu), axis=-1, keepdims=True)
    return ((xf - mu) * lax.rsqrt(var + LN_EPS) * g.astype(jnp.float32) + b.astype(jnp.float32)).astype(x.dtype)


def rms_norm(x, g):
    xf = x.astype(jnp.float32)
    return xf * lax.rsqrt(jnp.mean(jnp.square(xf), axis=-1, keepdims=True) + RMS_EPS) * g.astype(jnp.float32)


def swiglu(x, w1, w3, w2):
    return (jax.nn.silu(x @ w1) * (x @ w3)) @ w2


def alibi_slopes(n):
    return 2.0 ** (-8.0 * jnp.arange(1, n + 1, dtype=jnp.float32) / n)


def swa_sink_attention(q, k, v, sinks, slopes):
    B, S = q.shape[0], q.shape[1]
    W = SWA_WINDOW
    nb = S // W
    f32 = jnp.float32
    qb = q.reshape(B, nb, W, SWA_KV_HEADS, SWA_GROUP, HEAD_DIM).astype(f32)
    kb = k.reshape(B, nb, W, SWA_KV_HEADS, HEAD_DIM).astype(f32)
    vb = v.reshape(B, nb, W, SWA_KV_HEADS, HEAD_DIM).astype(f32)
    pad = ((0, 0), (1, 0), (0, 0), (0, 0), (0, 0))
    kk = jnp.concatenate([jnp.pad(kb, pad)[:, :-1], kb], axis=2)
    vv = jnp.concatenate([jnp.pad(vb, pad)[:, :-1], vb], axis=2)
    scores = jnp.einsum('bnqhgd,bnkhd->bnhgqk', qb, kk) * ATTN_SCALE
    qi = jnp.arange(W)[:, None]
    kj = jnp.arange(2 * W)[None, :]
    dist = qi + W - kj
    band = (dist >= 0) & (dist < W)
    not_before_start = (jnp.arange(nb)[:, None, None] > 0) | (kj[None] >= W)
    valid = band[None] & not_before_start
    logits = scores - slopes.astype(f32)[:, :, None, None] * dist.astype(f32)
    logits = jnp.where(valid[None, :, None, None], logits, -jnp.inf)
    sink = sinks.astype(f32)[None, None, :, :, None, None]
    m = jnp.maximum(jnp.max(logits, axis=-1, keepdims=True), sink)
    p = jnp.exp(logits - m)
    denom = jnp.sum(p, axis=-1, keepdims=True) + jnp.exp(sink - m)
    o = jnp.einsum('bnhgqk,bnkhd->bnqhgd', p / denom, vv)
    return o.reshape(B, S, SWA_KV_HEADS * SWA_GROUP * HEAD_DIM)


def moba_attention(q, k, v, slopes):
    B, S, H, Dh = q.shape
    f32 = jnp.float32
    Sp = -(-S // MOBA_BLOCK) * MOBA_BLOCK
    pad = ((0, 0), (0, Sp - S), (0, 0), (0, 0))
    q = jnp.pad(q, pad)
    k = jnp.pad(k, pad)
    v = jnp.pad(v, pad)
    nb = Sp // MOBA_BLOCK
    nq = Sp // MOBA_Q_CHUNK
    Q = MOBA_Q_CHUNK
    K = min(MOBA_TOPK, nb)
    qh = jnp.transpose(q, (0, 2, 1, 3)).astype(f32)
    kb = jnp.transpose(k, (0, 2, 1, 3)).astype(f32).reshape(B, H, nb, MOBA_BLOCK, Dh)
    vb = jnp.transpose(v, (0, 2, 1, 3)).astype(f32).reshape(B, H, nb, MOBA_BLOCK, Dh)
    kmean = jnp.mean(kb, axis=3)
    q_chunks = qh.reshape(B, H, nq, Q, Dh).transpose(2, 0, 1, 3, 4)
    bidx = jnp.arange(B)[:, None, None, None]
    hidx = jnp.arange(H)[None, :, None, None]
    blk_pos = jnp.arange(MOBA_BLOCK)
    slope = slopes.astype(f32)[None, :, None, None]

    def one_chunk(args):
        c, qc = args
        t = c * Q + jnp.arange(Q)
        own = (c * Q) // MOBA_BLOCK
        gate = jnp.einsum('bhqd,bhnd->bhqn', qc, kmean)
        gate = jnp.where(jnp.arange(nb) < own, gate, -jnp.inf)
        top_val, top_idx = lax.top_k(gate, K)
        sel_ok = top_val > -jnp.inf
        kg = kb[bidx, hidx, top_idx]
        vg = vb[bidx, hidx, top_idx]
        s_sel = jnp.einsum('bhqd,bhqjkd->bhqjk', qc, kg) * ATTN_SCALE
        pos_sel = top_idx[..., None] * MOBA_BLOCK + blk_pos
        d_sel = (t[None, None, :, None, None] - pos_sel).astype(f32)
        s_sel = jnp.where(sel_ok[..., None], s_sel - slope[..., None] * d_sel, -jnp.inf)
        k_own = lax.dynamic_index_in_dim(kb, own, axis=2, keepdims=False)
        v_own = lax.dynamic_index_in_dim(vb, own, axis=2, keepdims=False)
        s_own = jnp.einsum('bhqd,bhkd->bhqk', qc, k_own) * ATTN_SCALE
        d_own = t[:, None] - (own * MOBA_BLOCK + blk_pos)[None, :]
        s_own = jnp.where(d_own >= 0, s_own - slope * d_own.astype(f32), -jnp.inf)
        n_sel = K * MOBA_BLOCK
        logits = jnp.concatenate([s_sel.reshape(B, H, Q, n_sel), s_own], axis=-1)
        p = jax.nn.softmax(logits, axis=-1)
        p_sel = p[..., :n_sel].reshape(B, H, Q, K, MOBA_BLOCK)
        p_own = p[..., n_sel:]
        return (jnp.einsum('bhqjk,bhqjkd->bhqd', p_sel, vg)
                + jnp.einsum('bhqk,bhkd->bhqd', p_own, v_own))

    out = lax.map(one_chunk, (jnp.arange(nq), q_chunks))
    out = out.transpose(1, 0, 3, 2, 4).reshape(B, Sp, H * Dh)
    return out[:, :S]


def hybrid_attention(x, w_in, sinks, w_out):
    B, S, _ = x.shape
    proj = x @ w_in
    qa, ka, va, qb, kb, vb = jnp.split(proj, EVEN_SPLITS, axis=-1)
    slopes = alibi_slopes(SWA_Q_HEADS + MOBA_HEADS)
    oa = swa_sink_attention(
        qa.reshape(B, S, SWA_KV_HEADS, SWA_GROUP, HEAD_DIM),
        ka.reshape(B, S, SWA_KV_HEADS, HEAD_DIM),
        va.reshape(B, S, SWA_KV_HEADS, HEAD_DIM),
        sinks.reshape(SWA_KV_HEADS, SWA_GROUP),
        slopes[:SWA_Q_HEADS].reshape(SWA_KV_HEADS, SWA_GROUP))
    ob = moba_attention(
        qb.reshape(B, S, MOBA_HEADS, HEAD_DIM),
        kb.reshape(B, S, MOBA_HEADS, HEAD_DIM),
        vb.reshape(B, S, MOBA_HEADS, HEAD_DIM),
        slopes[SWA_Q_HEADS:])
    o = jnp.concatenate([oa, ob], axis=-1).astype(x.dtype)
    return o @ w_out


def hgrn2(x, w_in, norm_g, w_out, lb):
    B, S, _ = x.shape
    f32 = jnp.float32
    proj = x @ w_in
    q, f, i, g = jnp.split(proj, 4, axis=-1)
    ff = f.astype(f32)
    lbf = lb.astype(f32)
    log_f = jnp.logaddexp(jnp.log(lbf), jnp.log1p(-lbf) + jax.nn.log_sigmoid(ff))
    k = (1.0 - lbf) * jax.nn.sigmoid(-ff)
    q = jax.nn.silu(q.astype(f32))
    v = i.astype(f32)
    nc = S // HGRN_CHUNK
    C = HGRN_CHUNK

    def to_chunks(a, dim):
        return a.reshape(B, nc, C, HGRN_HEADS, dim).transpose(1, 0, 3, 2, 4)

    qs, ks, gs = to_chunks(q, HGRN_DK), to_chunks(k, HGRN_DK), to_chunks(log_f, HGRN_DK)
    vs = to_chunks(v, HGRN_DV)
    causal = jnp.tril(jnp.ones((C, C), dtype=bool))

    def step(state, inp):
        qc, kc, vc, gc = inp
        b = jnp.cumsum(gc, axis=2)
        o_inter = jnp.einsum('bhtk,bhkv->bhtv', qc * jnp.exp(b), state)
        diff = b[:, :, :, None, :] - b[:, :, None, :, :]
        decay = jnp.exp(jnp.where(causal[:, :, None], diff, -jnp.inf))
        a = jnp.einsum('bhtk,bhsk,bhtsk->bhts', qc, kc, decay)
        o_intra = jnp.einsum('bhts,bhsv->bhtv', a, vc)
        b_last = b[:, :, -1:, :]
        new_state = (jnp.exp(b_last[:, :, 0, :])[..., None] * state
                     + jnp.einsum('bhsk,bhsv->bhkv', kc * jnp.exp(b_last - b), vc))
        return new_state, o_inter + o_intra

    state0 = jnp.zeros((B, HGRN_HEADS, HGRN_DK, HGRN_DV), f32)
    _, o = lax.scan(step, state0, (qs, ks, vs, gs))
    o = o.transpose(1, 0, 3, 2, 4).reshape(B, S, HGRN_HEADS, HGRN_DV)
    o = rms_norm(o, norm_g).reshape(B, S, HGRN_WIDTH) * jax.nn.silu(g.astype(f32))
    return o.astype(x.dtype) @ w_out


def setup_inputs(seed: int = 0) -> dict:
    key = jax.random.key(seed)
    ks = jax.random.split(key, 13)
    f32 = jnp.float32

    def nrm(k, shape, scale):
        return jax.random.normal(k, shape, f32) * scale

    return {
        "x": nrm(ks[0], (BATCH, SEQ, D_MODEL), 1.0),
        "ffn_w1": nrm(ks[1], (DEPTH, 2, D_MODEL, D_FF), D_MODEL ** -0.5),
        "ffn_w3": nrm(ks[2], (DEPTH, 2, D_MODEL, D_FF), D_MODEL ** -0.5),
        "ffn_w2": nrm(ks[3], (DEPTH, 2, D_FF, D_MODEL), D_FF ** -0.5 * DEEPNORM_BETA),
        "ln_g": 1.0 + nrm(ks[4], (DEPTH, 3, D_MODEL), 0.02),
        "ln_b": nrm(ks[5], (DEPTH, 3, D_MODEL), 0.02),
        "attn_w_in": nrm(ks[6], (N_EVEN, D_MODEL, EVEN_IN_WIDTH), D_MODEL ** -0.5),
        "attn_sinks": nrm(ks[7], (N_EVEN, SWA_Q_HEADS), 1.0),
        "attn_w_out": nrm(ks[8], (N_EVEN, EVEN_OUT_WIDTH, D_MODEL), EVEN_OUT_WIDTH ** -0.5 * DEEPNORM_BETA),
        "hgrn_w_in": nrm(ks[9], (N_ODD, D_MODEL, 4 * HGRN_WIDTH), D_MODEL ** -0.5),
        "hgrn_norm_g": 1.0 + nrm(ks[10], (N_ODD, HGRN_DV), 0.02),
        "hgrn_w_out": nrm(ks[11], (N_ODD, HGRN_WIDTH, D_MODEL), HGRN_WIDTH ** -0.5 * DEEPNORM_BETA),
        "hgrn_lower_bounds": nrm(ks[12], (DEPTH, HGRN_WIDTH), 0.1),
    }


def reference(x, ffn_w1, ffn_w3, ffn_w2, ln_g, ln_b, attn_w_in, attn_sinks, attn_w_out,
              hgrn_w_in, hgrn_norm_g, hgrn_w_out, hgrn_lower_bounds):
    lbs = jnp.cumsum(jax.nn.softmax(hgrn_lower_bounds.astype(jnp.float32), axis=0), axis=0)
    lbs = lbs - lbs[0:1]
    for l in range(DEPTH):
        j = l // 2
        h = swiglu(x, ffn_w1[l, 0], ffn_w3[l, 0], ffn_w2[l, 0])
        x = layer_norm(DEEPNORM_ALPHA * x + FFN_RES_WEIGHT * h, ln_g[l, 0], ln_b[l, 0])
        if l % 2 == 0:
            h = hybrid_attention(x, attn_w_in[j], attn_sinks[j], attn_w_out[j])
        else:
            h = hgrn2(x, hgrn_w_in[j], hgrn_norm_g[j], hgrn_w_out[j], lbs[l])
        x = layer_norm(DEEPNORM_ALPHA * x + h, ln_g[l, 1], ln_b[l, 1])
        h = swiglu(x, ffn_w1[l, 1], ffn_w3[l, 1], ffn_w2[l, 1])
        x = layer_norm(DEEPNORM_ALPHA * x + FFN_RES_WEIGHT * h, ln_g[l, 2], ln_b[l, 2])
    return x
```

```python
import functools

import jax
import jax.numpy as jnp
from jax import lax
from jax.experimental import pallas as pl
from jax.experimental.pallas import tpu as pltpu

F32 = jnp.float32
BF16 = jnp.bfloat16

DEPTH = 2
HEAD_DIM = 64
SWA_Q_HEADS = 8
SWA_KV_HEADS = 2
SWA_GROUP = SWA_Q_HEADS // SWA_KV_HEADS
SWA_WINDOW = 128
MOBA_HEADS = 8
MOBA_BLOCK = 256
MOBA_TOPK = 3
HGRN_HEADS = 8
HGRN_DK = 128
HGRN_DV = 128
DEEPNORM_ALPHA = (2 * DEPTH) ** 0.25
FFN_RES_WEIGHT = 0.5
LN_EPS = 1e-5
RMS_EPS = 1e-6
ATTN_SCALE = HEAD_DIM ** -0.5
N_ATTN_HEADS = SWA_Q_HEADS + MOBA_HEADS

V7X_LANES = 128
V7X_VMEM_BYTES = 64 * 1024 * 1024
VMEM_LIMIT_BYTES = 56 * 1024 * 1024

FFN_CHUNK = 256
HGRN_CHUNK = 256
MASKED_SCORE = -1e30


def _alibi_slope(i):
    return float(2.0 ** (-8.0 * (i + 1) / N_ATTN_HEADS))


def _token_tile(t, cap):
    tm = min(t, cap)
    assert t % tm == 0, (t, tm)
    return tm


def _params(*sem):
    return pltpu.CompilerParams(dimension_semantics=sem, vmem_limit_bytes=VMEM_LIMIT_BYTES)


def _const_spec(shape):
    return pl.BlockSpec(shape, lambda *_: (0,) * len(shape), pipeline_mode=pl.Buffered(1))


def _layer_norm_rows(y, g, b):
    mu = jnp.mean(y, axis=-1, keepdims=True)
    d = y - mu
    var = jnp.mean(d * d, axis=-1, keepdims=True)
    return d * lax.rsqrt(var + LN_EPS) * g + b


def _ffn_ln_kernel(x_ref, w1_ref, w3_ref, w2_ref, g_ref, b_ref, o_ref, xb_ref, acc_ref, *, n_chunks):
    x = x_ref[...]
    xb_ref[...] = x.astype(BF16)
    acc_ref[...] = jnp.zeros_like(acc_ref)

    def chunk(c, carry):
        xb = xb_ref[...]
        h1 = jnp.dot(xb, w1_ref[c], preferred_element_type=F32)
        h3 = jnp.dot(xb, w3_ref[c], preferred_element_type=F32)
        h = (h1 * jax.nn.sigmoid(h1)) * h3
        acc_ref[...] += jnp.dot(h.astype(BF16), w2_ref[c], preferred_element_type=F32)
        return carry

    lax.fori_loop(0, n_chunks, chunk, 0)
    y = DEEPNORM_ALPHA * x + FFN_RES_WEIGHT * acc_ref[...]
    o_ref[...] = _layer_norm_rows(y, g_ref[...], b_ref[...])


def _ffn_ln(x, w1, w3, w2, g, b):
    t, d = x.shape
    d_ff = w1.shape[1]
    assert d_ff % FFN_CHUNK == 0
    n_chunks = d_ff // FFN_CHUNK
    w1c = w1.astype(BF16).reshape(d, n_chunks, FFN_CHUNK).transpose(1, 0, 2)
    w3c = w3.astype(BF16).reshape(d, n_chunks, FFN_CHUNK).transpose(1, 0, 2)
    w2c = w2.astype(BF16).reshape(n_chunks, FFN_CHUNK, d)
    tm = _token_tile(t, 512)
    return pl.pallas_call(
        functools.partial(_ffn_ln_kernel, n_chunks=n_chunks),
        grid=(t // tm,),
        in_specs=[
            pl.BlockSpec((tm, d), lambda i: (i, 0)),
            _const_spec((n_chunks, d, FFN_CHUNK)),
            _const_spec((n_chunks, d, FFN_CHUNK)),
            _const_spec((n_chunks, FFN_CHUNK, d)),
            _const_spec((1, d)),
            _const_spec((1, d)),
        ],
        out_specs=pl.BlockSpec((tm, d), lambda i: (i, 0)),
        out_shape=jax.ShapeDtypeStruct((t, d), F32),
        scratch_shapes=[pltpu.VMEM((tm, d), BF16), pltpu.VMEM((tm, d), F32)],
        compiler_params=_params("parallel"),
        name="ffn_ln",
    )(x, w1c, w3c, w2c, g.reshape(1, d), b.reshape(1, d))


def _proj_kernel(x_ref, w_ref, o_ref, *, col_chunk):
    xb = x_ref[...].astype(BF16)
    n = w_ref.shape[1]
    for c0 in range(0, n, col_chunk):
        o_ref[:, c0:c0 + col_chunk] = jnp.dot(
            xb, w_ref[:, c0:c0 + col_chunk], preferred_element_type=F32).astype(o_ref.dtype)


def _proj(x, w, out_dtype):
    t, d = x.shape
    n = w.shape[1]
    col_chunk = 256
    assert n % col_chunk == 0
    tm = _token_tile(t, 512)
    return pl.pallas_call(
        functools.partial(_proj_kernel, col_chunk=col_chunk),
        grid=(t // tm,),
        in_specs=[pl.BlockSpec((tm, d), lambda i: (i, 0)), _const_spec((d, n))],
        out_specs=pl.BlockSpec((tm, n), lambda i: (i, 0)),
        out_shape=jax.ShapeDtypeStruct((t, n), out_dtype),
        compiler_params=_params("parallel"),
        name="in_proj",
    )(x, w.astype(BF16))


def _out_proj_ln_kernel(x_ref, o_ref, w_ref, g_ref, b_ref, y_ref):
    h = jnp.dot(o_ref[...], w_ref[...], preferred_element_type=F32)
    y = DEEPNORM_ALPHA * x_ref[...] + h
    y_ref[...] = _layer_norm_rows(y, g_ref[...], b_ref[...])


def _out_proj_ln(x, o, w_out, g, b):
    t, d = x.shape
    k = o.shape[1]
    tm = _token_tile(t, 512)
    return pl.pallas_call(
        _out_proj_ln_kernel,
        grid=(t // tm,),
        in_specs=[
            pl.BlockSpec((tm, d), lambda i: (i, 0)),
            pl.BlockSpec((tm, k), lambda i: (i, 0)),
            _const_spec((k, d)),
            _const_spec((1, d)),
            _const_spec((1, d)),
        ],
        out_specs=pl.BlockSpec((tm, d), lambda i: (i, 0)),
        out_shape=jax.ShapeDtypeStruct((t, d), F32),
        compiler_params=_params("parallel"),
        name="out_proj_ln",
    )(x, o, w_out.astype(BF16), g.reshape(1, d), b.reshape(1, d))


def _swa_kernel(sinks_ref, q_ref, kp_ref, kc_ref, vp_ref, vc_ref, o_ref):
    i = pl.program_id(1)
    w = SWA_WINDOW
    kk = jnp.concatenate([kp_ref[0], kc_ref[0]], axis=0)
    vv = jnp.concatenate([vp_ref[0], vc_ref[0]], axis=0)
    qi = lax.broadcasted_iota(jnp.int32, (w, 2 * w), 0)
    kj = lax.broadcasted_iota(jnp.int32, (w, 2 * w), 1)
    dist = qi + w - kj
    valid = (dist >= 0) & (dist < w) & ((i > 0) | (kj >= w))
    distf = dist.astype(F32)
    low_half = lax.broadcasted_iota(jnp.int32, (w, V7X_LANES), 1) < HEAD_DIM
    n_pairs = SWA_Q_HEADS // 2
    for p in range(n_pairs):
        qp = q_ref[0, :, p * V7X_LANES:(p + 1) * V7X_LANES]
        outs = []
        for a in range(2):
            h = a * SWA_GROUP + p
            qm = jnp.where(low_half if a == 0 else jnp.logical_not(low_half), qp, jnp.zeros_like(qp))
            s = lax.dot_general(qm, kk, (((1,), (1,)), ((), ())), preferred_element_type=F32) * ATTN_SCALE
            logits = jnp.where(valid, s - _alibi_slope(h) * distf, -jnp.inf)
            sink = sinks_ref[h]
            m = jnp.maximum(jnp.max(logits, axis=-1, keepdims=True), sink)
            pexp = jnp.exp(logits - m)
            denom = jnp.sum(pexp, axis=-1, keepdims=True) + jnp.exp(sink - m)
            outs.append(jnp.dot(pexp.astype(BF16), vv, preferred_element_type=F32) / denom)
        o_ref[0, :, p * V7X_LANES:(p + 1) * V7X_LANES] = jnp.where(low_half, outs[0], outs[1]).astype(o_ref.dtype)


def _swa(proj, sinks, col0):
    b, s, _ = proj.shape
    w = SWA_WINDOW
    assert s % w == 0 and col0 % (SWA_Q_HEADS * HEAD_DIM) == 0
    qw = SWA_Q_HEADS * HEAD_DIM
    qblk = col0 // qw
    kblk = (col0 + qw) // V7X_LANES
    vblk = kblk + 1
    return pl.pallas_call(
        _swa_kernel,
        grid=(b, s // w),
        in_specs=[
            pl.BlockSpec(memory_space=pltpu.SMEM),
            pl.BlockSpec((1, w, qw), lambda bi, i: (bi, i, qblk)),
            pl.BlockSpec((1, w, V7X_LANES), lambda bi, i: (bi, jnp.maximum(i - 1, 0), kblk)),
            pl.BlockSpec((1, w, V7X_LANES), lambda bi, i: (bi, i, kblk)),
            pl.BlockSpec((1, w, V7X_LANES), lambda bi, i: (bi, jnp.maximum(i - 1, 0), vblk)),
            pl.BlockSpec((1, w, V7X_LANES), lambda bi, i: (bi, i, vblk)),
        ],
        out_specs=pl.BlockSpec((1, w, qw), lambda bi, i: (bi, i, 0)),
        out_shape=jax.ShapeDtypeStruct((b, s, qw), BF16),
        compiler_params=_params("parallel", "parallel"),
        name="swa",
    )(sinks.astype(F32), proj, proj, proj, proj, proj)


def _moba_kernel(slopes_ref, q_ref, k_ref, v_ref, o_ref, kmean_ref, lhs_ref, m_ref, l_ref, acc_ref, *, n_blocks):
    pair = pl.program_id(1)
    i = pl.program_id(2)
    blk = MOBA_BLOCK
    lanes = V7X_LANES
    n_sel = min(MOBA_TOPK, n_blocks)

    @pl.when(i == 0)
    def _():
        kmean_ref[...] = jnp.zeros_like(kmean_ref)

    qp = q_ref[0]
    row0 = pl.multiple_of(i * blk, blk)
    k_own = k_ref[0, pl.ds(row0, blk), :]
    v_own = v_ref[0, pl.ds(row0, blk), :]
    lane_id = lax.broadcasted_iota(jnp.int32, (blk, lanes), 1)
    low_half = lane_id < HEAD_DIM
    lane_f = lane_id.astype(F32)
    r_id = lax.broadcasted_iota(jnp.int32, (blk, blk), 0)
    c_id = lax.broadcasted_iota(jnp.int32, (blk, blk), 1)
    rel = (r_id - c_id).astype(F32)
    causal = c_id <= r_id
    kmean_b = kmean_ref[...].astype(BF16)
    slopes = [slopes_ref[2 * pair + a] for a in range(2)]

    for a in range(2):
        qm = jnp.where(low_half if a == 0 else jnp.logical_not(low_half), qp, jnp.zeros_like(qp))
        gate = lax.dot_general(qm, kmean_b, (((1,), (1,)), ((), ())), preferred_element_type=F32)
        gate = jnp.where(lane_id < i, gate, -jnp.inf)
        sel_code = jnp.full((blk, lanes), MASKED_SCORE, F32)
        for _ in range(n_sel):
            mx = jnp.max(gate, axis=-1, keepdims=True)
            first = jnp.min(jnp.where(gate == mx, lane_f, float(lanes)), axis=-1, keepdims=True)
            pick = (lane_f == first) & (mx > -jnp.inf)
            sel_code = jnp.where(pick, 0.0, sel_code)
            gate = jnp.where(pick, -jnp.inf, gate)
        q_scaled = (qm.astype(F32) * ATTN_SCALE).astype(BF16)
        lhs_ref[a] = jnp.concatenate([q_scaled, sel_code.astype(BF16)], axis=1)
        s = lax.dot_general(q_scaled, k_own, (((1,), (1,)), ((), ())), preferred_element_type=F32)
        logits = jnp.where(causal, s - slopes[a] * rel, -jnp.inf)
        m = jnp.max(logits, axis=-1, keepdims=True)
        p = jnp.exp(logits - m)
        m_ref[a] = jnp.broadcast_to(m, (blk, lanes))
        l_ref[a] = jnp.broadcast_to(jnp.sum(p, axis=-1, keepdims=True), (blk, lanes))
        acc_ref[a] = jnp.dot(p.astype(BF16), v_own, preferred_element_type=F32)

    def past_block(j, carry):
        r0 = pl.multiple_of(j * blk, blk)
        kj = k_ref[0, pl.ds(r0, blk), :]
        vj = v_ref[0, pl.ds(r0, blk), :]
        one_hot = jnp.where(lane_id == j, 1.0, 0.0).astype(BF16)
        rhs = jnp.concatenate([kj, one_hot], axis=1)
        block_dist = ((i - j) * blk).astype(F32)
        for a in range(2):
            s = lax.dot_general(lhs_ref[a], rhs, (((1,), (1,)), ((), ())), preferred_element_type=F32)
            logits = s - slopes[a] * (rel + block_dist)
            m_old = m_ref[a]
            m_new = jnp.maximum(m_old, jnp.max(logits, axis=-1, keepdims=True))
            alpha = jnp.exp(m_old - m_new)
            p = jnp.exp(logits - jnp.concatenate([m_new, m_new], axis=1))
            l_ref[a] = alpha * l_ref[a] + jnp.sum(p, axis=-1, keepdims=True)
            acc_ref[a] = alpha * acc_ref[a] + jnp.dot(p.astype(BF16), vj, preferred_element_type=F32)
            m_ref[a] = m_new
        return carry

    lax.fori_loop(0, i, past_block, 0)

    o_ref[0] = jnp.where(low_half, acc_ref[0] / l_ref[0], acc_ref[1] / l_ref[1]).astype(o_ref.dtype)
    kmean_ref[pl.ds(i, 1), :] = jnp.mean(k_own.astype(F32), axis=0, keepdims=True)


def _moba(proj, col0):
    b, s, _ = proj.shape
    blk = MOBA_BLOCK
    assert s % blk == 0 and col0 % V7X_LANES == 0
    n_blocks = s // blk
    assert n_blocks <= V7X_LANES
    n_pairs = MOBA_HEADS // 2
    hw = MOBA_HEADS * HEAD_DIM
    qb, kb, vb = col0 // V7X_LANES, (col0 + hw) // V7X_LANES, (col0 + 2 * hw) // V7X_LANES
    slopes = jnp.asarray([_alibi_slope(SWA_Q_HEADS + h) for h in range(MOBA_HEADS)], dtype=F32)
    return pl.pallas_call(
        functools.partial(_moba_kernel, n_blocks=n_blocks),
        grid=(b, n_pairs, n_blocks),
        in_specs=[
            pl.BlockSpec(memory_space=pltpu.SMEM),
            pl.BlockSpec((1, blk, V7X_LANES), lambda bi, p, i: (bi, i, qb + p)),
            pl.BlockSpec((1, s, V7X_LANES), lambda bi, p, i: (bi, 0, kb + p)),
            pl.BlockSpec((1, s, V7X_LANES), lambda bi, p, i: (bi, 0, vb + p)),
        ],
        out_specs=pl.BlockSpec((1, blk, V7X_LANES), lambda bi, p, i: (bi, i, p)),
        out_shape=jax.ShapeDtypeStruct((b, s, hw), BF16),
        scratch_shapes=[
            pltpu.VMEM((V7X_LANES, V7X_LANES), F32),
            pltpu.VMEM((2, blk, 2 * V7X_LANES), BF16),
            pltpu.VMEM((2, blk, V7X_LANES), F32),
            pltpu.VMEM((2, blk, V7X_LANES), F32),
            pltpu.VMEM((2, blk, V7X_LANES), F32),
        ],
        compiler_params=_params("arbitrary", "arbitrary", "arbitrary"),
        name="moba",
    )(slopes, proj, proj, proj)


def _block_ref_rows(b, m):
    c, n = b.shape
    if 2 * m >= 8:
        b3 = b.reshape(c // (2 * m), 2 * m, n)
        return jnp.broadcast_to(b3[:, m - 1:m, :], b3.shape).reshape(c, n)
    r = lax.broadcasted_iota(jnp.int32, (c, n), 0)
    if m == 1:
        return jnp.where(r % 2 == 1, pltpu.roll(b, 1, 0), b)
    assert m == 2
    q4 = r % 4
    up1 = pltpu.roll(b, c - 1, 0)
    dn1 = pltpu.roll(b, 1, 0)
    dn2 = pltpu.roll(b, 2, 0)
    return jnp.where(q4 == 0, up1, jnp.where(q4 == 1, b, jnp.where(q4 == 2, dn1, dn2)))


def _hgrn_kernel(lbp_ref, ng_ref, q_ref, f_ref, i_ref, g_ref, o_ref, st_ref, lv_ref, *, layer, chunk):
    c = chunk
    ci = pl.program_id(2)
    t_id = lax.broadcasted_iota(jnp.int32, (c, c), 0)
    s_id = lax.broadcasted_iota(jnp.int32, (c, c), 1)

    @pl.when(ci == 0)
    def _():
        st_ref[...] = jnp.zeros_like(st_ref)
        x = t_id ^ s_id
        code = jnp.zeros((c, c), jnp.int32)
        bit = 1
        while bit < c:
            code = code + (x >= bit).astype(jnp.int32)
            bit *= 2
        lv_ref[...] = jnp.where(s_id <= t_id, code, -1)

    lbp = lbp_ref[...].astype(F32)
    e = jnp.exp(lbp - jnp.max(lbp, axis=0, keepdims=True))
    sm = e / jnp.sum(e, axis=0, keepdims=True)
    lb = jnp.sum(sm[1:layer + 1], axis=0, keepdims=True) if layer >= 1 else jnp.zeros((1, lbp.shape[1]), F32)

    qraw = q_ref[0].astype(F32)
    fraw = f_ref[0].astype(F32)
    v = i_ref[0].astype(F32)
    graw = g_ref[0].astype(F32)
    q = qraw * jax.nn.sigmoid(qraw)
    f_t = lb + (1.0 - lb) * jax.nn.sigmoid(fraw)
    g = jnp.log(f_t)
    kk = (1.0 - lb) * jax.nn.sigmoid(-fraw)

    tril = jnp.where(s_id <= t_id, 1.0, 0.0).astype(BF16)
    g_hi = g.astype(BF16)
    r1 = g - g_hi.astype(F32)
    g_mid = r1.astype(BF16)
    g_lo = (r1 - g_mid.astype(F32)).astype(BF16)
    b = (jnp.dot(tril, g_hi, preferred_element_type=F32) + jnp.dot(tril, g_mid, preferred_element_type=F32)
         + jnp.dot(tril, g_lo, preferred_element_type=F32))

    nt = (((1,), (1,)), ((), ()))
    st = st_ref[...]
    o = lax.dot_general((q * jnp.exp(b)).astype(BF16), st.astype(BF16), nt, preferred_element_type=F32)

    lv = lv_ref[...]
    a = jnp.where(lv == 0, lax.dot_general(q.astype(BF16), kk.astype(BF16), nt, preferred_element_type=F32), 0.0)
    m, code = 1, 1
    while m < c:
        dec = jnp.exp(-jnp.abs(b - _block_ref_rows(b, m)))
        p = lax.dot_general((q * dec).astype(BF16), (kk * dec).astype(BF16), nt, preferred_element_type=F32)
        a = jnp.where(lv == code, p, a)
        m, code = m * 2, code + 1
    o = o + jnp.dot(a.astype(BF16), v.astype(BF16), preferred_element_type=F32)

    b_last = b[c - 1:c, :]
    k_end = (kk * jnp.exp(b_last - b)).astype(BF16)
    st_ref[...] = st * jnp.exp(b_last) + jnp.dot(v.T.astype(BF16), k_end, preferred_element_type=F32)

    rms = o * lax.rsqrt(jnp.mean(o * o, axis=-1, keepdims=True) + RMS_EPS) * ng_ref[...].astype(F32)
    o_ref[0] = (rms * (graw * jax.nn.sigmoid(graw))).astype(o_ref.dtype)


def _hgrn(proj, lower_bounds, norm_g, layer):
    b, s, _ = proj.shape
    c = min(HGRN_CHUNK, s)
    assert s % c == 0 and c % 8 == 0 and (c & (c - 1)) == 0
    h = HGRN_HEADS
    depth = lower_bounds.shape[0]

    def col(j):
        return lambda bi, hi, ci: (bi, ci, hi + j * h)

    return pl.pallas_call(
        functools.partial(_hgrn_kernel, layer=layer, chunk=c),
        grid=(b, h, s // c),
        in_specs=[
            pl.BlockSpec((depth, HGRN_DK), lambda bi, hi, ci: (0, hi)),
            pl.BlockSpec((1, HGRN_DV), lambda bi, hi, ci: (0, 0)),
            pl.BlockSpec((1, c, HGRN_DK), col(0)),
            pl.BlockSpec((1, c, HGRN_DK), col(1)),
            pl.BlockSpec((1, c, HGRN_DV), col(2)),
            pl.BlockSpec((1, c, HGRN_DK), col(3)),
        ],
        out_specs=pl.BlockSpec((1, c, HGRN_DV), lambda bi, hi, ci: (bi, ci, hi)),
        out_shape=jax.ShapeDtypeStruct((b, s, h * HGRN_DV), BF16),
        scratch_shapes=[pltpu.VMEM((HGRN_DV, HGRN_DK), F32), pltpu.VMEM((c, c), jnp.int32)],
        compiler_params=_params("arbitrary", "arbitrary", "arbitrary"),
        name="hgrn",
    )(lower_bounds, norm_g.reshape(1, HGRN_DV), proj, proj, proj, proj)


def _swa_head_order():
    order = []
    for p in range(SWA_GROUP):
        for a in range(SWA_KV_HEADS):
            h = a * SWA_GROUP + p
            order.extend(range(h * HEAD_DIM, (h + 1) * HEAD_DIM))
    return jnp.asarray(order, dtype=jnp.int32)


def _attention_layer(x, batch, w_in, sinks, w_out, g, b):
    t, d = x.shape
    s = t // batch
    perm = _swa_head_order()
    qa_w = SWA_Q_HEADS * HEAD_DIM
    w_in_p = jnp.concatenate([w_in[:, :qa_w][:, perm], w_in[:, qa_w:]], axis=1)
    w_out_p = jnp.concatenate([w_out[:qa_w][perm], w_out[qa_w:]], axis=0)
    proj = _proj(x, w_in_p, BF16).reshape(batch, s, -1)
    oa = _swa(proj, sinks, 0)
    ob = _moba(proj, qa_w + 2 * SWA_KV_HEADS * HEAD_DIM)
    o = jnp.concatenate([oa, ob], axis=-1).reshape(t, -1)
    return _out_proj_ln(x, o, w_out_p, g, b)


def _hgrn_layer(x, batch, w_in, norm_g, w_out, lower_bounds, layer, g, b):
    t, d = x.shape
    s = t // batch
    proj = _proj(x, w_in, F32).reshape(batch, s, -1)
    o = _hgrn(proj, lower_bounds, norm_g, layer).reshape(t, -1)
    return _out_proj_ln(x, o, w_out, g, b)


def kernel(x, ffn_w1, ffn_w3, ffn_w2, ln_g, ln_b, attn_w_in, attn_sinks, attn_w_out,
           hgrn_w_in, hgrn_norm_g, hgrn_w_out, hgrn_lower_bounds):
    batch, seq, d = x.shape
    xf = x.reshape(batch * seq, d)
    depth = ffn_w1.shape[0]
    for l in range(depth):
        j = l // 2
        xf = _ffn_ln(xf, ffn_w1[l, 0], ffn_w3[l, 0], ffn_w2[l, 0], ln_g[l, 0], ln_b[l, 0])
        if l % 2 == 0:
            xf = _attention_layer(xf, batch, attn_w_in[j], attn_sinks[j], attn_w_out[j], ln_g[l, 1], ln_b[l, 1])
        else:
            xf = _hgrn_layer(xf, batch, hgrn_w_in[j], hgrn_norm_g[j], hgrn_w_out[j], hgrn_lower_bounds, l,
                             ln_g[l, 1], ln_b[l, 1])
        xf = _ffn_ln(xf, ffn_w1[l, 1], ffn_w3[l, 1], ffn_w2[l, 1], ln_g[l, 2], ln_b[l, 2])
    return xf.reshape(batch, seq, d)
```

```python
import functools

import jax
import jax.numpy as jnp
from jax import lax
from jax.experimental import pallas as pl
from jax.experimental.pallas import tpu as pltpu

F32 = jnp.float32
BF16 = jnp.bfloat16

DEPTH = 2
HEAD_DIM = 64
SWA_Q_HEADS = 8
SWA_KV_HEADS = 2
SWA_GROUP = SWA_Q_HEADS // SWA_KV_HEADS
SWA_WINDOW = 128
MOBA_HEADS = 8
MOBA_BLOCK = 256
MOBA_TOPK = 3
HGRN_HEADS = 8
HGRN_DK = 128
HGRN_DV = 128
DEEPNORM_ALPHA = (2 * DEPTH) ** 0.25
FFN_RES_WEIGHT = 0.5
LN_EPS = 1e-5
RMS_EPS = 1e-6
ATTN_SCALE = HEAD_DIM ** -0.5
LOG2_E = 1.4426950408889634
N_ATTN_HEADS = SWA_Q_HEADS + MOBA_HEADS

V7X_LANES = 128
V7X_VMEM_BYTES = 64 * 1024 * 1024
VMEM_LIMIT_BYTES = 56 * 1024 * 1024

FFN_CHUNK = 256
HGRN_CHUNK = 256
MOBA_GROUP = 4
MASKED_SCORE = -1e30


def _alibi_slope(i):
    return float(2.0 ** (-8.0 * (i + 1) / N_ATTN_HEADS))


def _token_tile(t, cap):
    tm = min(t, cap)
    assert t % tm == 0, (t, tm)
    return tm


def _params(*sem):
    return pltpu.CompilerParams(dimension_semantics=sem, vmem_limit_bytes=VMEM_LIMIT_BYTES)


def _const_spec(shape):
    return pl.BlockSpec(shape, lambda *_: (0,) * len(shape), pipeline_mode=pl.Buffered(1))


def _bf16_truncate(x):
    bits = lax.bitcast_convert_type(x, jnp.uint32) & jnp.uint32(0xFFFF0000)
    return lax.bitcast_convert_type(bits, F32)


def _layer_norm_rows(y, g, b):
    mu = jnp.mean(y, axis=-1, keepdims=True)
    d = y - mu
    var = jnp.mean(d * d, axis=-1, keepdims=True)
    return d * lax.rsqrt(var + LN_EPS) * g + b


def _ffn_ln_kernel(x_ref, w1_ref, w3_ref, w2_ref, g_ref, b_ref, o_ref, xb_ref, acc_ref, *, n_chunks):
    x = x_ref[...]
    xb_ref[...] = x.astype(BF16)
    acc_ref[...] = jnp.zeros_like(acc_ref)

    def chunk(c, carry):
        xb = xb_ref[...]
        h1 = jnp.dot(xb, w1_ref[c], preferred_element_type=F32)
        h3 = jnp.dot(xb, w3_ref[c], preferred_element_type=F32)
        h = (h1 * jax.nn.sigmoid(h1)) * h3
        acc_ref[...] += jnp.dot(h.astype(BF16), w2_ref[c], preferred_element_type=F32)
        return carry

    lax.fori_loop(0, n_chunks, chunk, 0)
    y = DEEPNORM_ALPHA * x + FFN_RES_WEIGHT * acc_ref[...]
    o_ref[...] = _layer_norm_rows(y, g_ref[...], b_ref[...])


def _ffn_ln(x, w1, w3, w2, g, b):
    t, d = x.shape
    d_ff = w1.shape[1]
    assert d_ff % FFN_CHUNK == 0
    n_chunks = d_ff // FFN_CHUNK
    w1c = w1.astype(BF16).reshape(d, n_chunks, FFN_CHUNK).transpose(1, 0, 2)
    w3c = w3.astype(BF16).reshape(d, n_chunks, FFN_CHUNK).transpose(1, 0, 2)
    w2c = w2.astype(BF16).reshape(n_chunks, FFN_CHUNK, d)
    tm = _token_tile(t, 512)
    return pl.pallas_call(
        functools.partial(_ffn_ln_kernel, n_chunks=n_chunks),
        grid=(t // tm,),
        in_specs=[
            pl.BlockSpec((tm, d), lambda i: (i, 0)),
            _const_spec((n_chunks, d, FFN_CHUNK)),
            _const_spec((n_chunks, d, FFN_CHUNK)),
            _const_spec((n_chunks, FFN_CHUNK, d)),
            _const_spec((1, d)),
            _const_spec((1, d)),
        ],
        out_specs=pl.BlockSpec((tm, d), lambda i: (i, 0)),
        out_shape=jax.ShapeDtypeStruct((t, d), F32),
        scratch_shapes=[pltpu.VMEM((tm, d), BF16), pltpu.VMEM((tm, d), F32)],
        compiler_params=_params("parallel"),
        name="ffn_ln",
    )(x, w1c, w3c, w2c, g.reshape(1, d), b.reshape(1, d))


def _proj_kernel(x_ref, w_ref, o_ref, *, col_chunk):
    xb = x_ref[...].astype(BF16)
    n = w_ref.shape[1]
    for c0 in range(0, n, col_chunk):
        o_ref[:, c0:c0 + col_chunk] = jnp.dot(
            xb, w_ref[:, c0:c0 + col_chunk], preferred_element_type=F32).astype(o_ref.dtype)


def _proj(x, w, out_dtype):
    t, d = x.shape
    n = w.shape[1]
    col_chunk = 256
    assert n % col_chunk == 0
    tm = _token_tile(t, 512)
    return pl.pallas_call(
        functools.partial(_proj_kernel, col_chunk=col_chunk),
        grid=(t // tm,),
        in_specs=[pl.BlockSpec((tm, d), lambda i: (i, 0)), _const_spec((d, n))],
        out_specs=pl.BlockSpec((tm, n), lambda i: (i, 0)),
        out_shape=jax.ShapeDtypeStruct((t, n), out_dtype),
        compiler_params=_params("parallel"),
        name="in_proj",
    )(x, w.astype(BF16))


def _out_proj_ln_kernel(x_ref, o_ref, w_ref, g_ref, b_ref, y_ref):
    h = jnp.dot(o_ref[...], w_ref[...], preferred_element_type=F32)
    y = DEEPNORM_ALPHA * x_ref[...] + h
    y_ref[...] = _layer_norm_rows(y, g_ref[...], b_ref[...])


def _out_proj_ln(x, o, w_out, g, b):
    t, d = x.shape
    k = o.shape[1]
    tm = _token_tile(t, 512)
    return pl.pallas_call(
        _out_proj_ln_kernel,
        grid=(t // tm,),
        in_specs=[
            pl.BlockSpec((tm, d), lambda i: (i, 0)),
            pl.BlockSpec((tm, k), lambda i: (i, 0)),
            _const_spec((k, d)),
            _const_spec((1, d)),
            _const_spec((1, d)),
        ],
        out_specs=pl.BlockSpec((tm, d), lambda i: (i, 0)),
        out_shape=jax.ShapeDtypeStruct((t, d), F32),
        compiler_params=_params("parallel"),
        name="out_proj_ln",
    )(x, o, w_out.astype(BF16), g.reshape(1, d), b.reshape(1, d))


def _swa_kernel(sinks_ref, q_ref, kp_ref, kc_ref, vp_ref, vc_ref, o_ref):
    i = pl.program_id(1)
    w = SWA_WINDOW
    kk = jnp.concatenate([kp_ref[0], kc_ref[0]], axis=0)
    vv = jnp.concatenate([vp_ref[0], vc_ref[0]], axis=0)
    qi = lax.broadcasted_iota(jnp.int32, (w, 2 * w), 0)
    kj = lax.broadcasted_iota(jnp.int32, (w, 2 * w), 1)
    dist = qi + w - kj
    valid = (dist >= 0) & (dist < w) & ((i > 0) | (kj >= w))
    distf = dist.astype(F32)
    low_half = lax.broadcasted_iota(jnp.int32, (w, V7X_LANES), 1) < HEAD_DIM
    n_pairs = SWA_Q_HEADS // 2
    for p in range(n_pairs):
        qp = q_ref[0, :, p * V7X_LANES:(p + 1) * V7X_LANES]
        outs = []
        for a in range(2):
            h = a * SWA_GROUP + p
            qm = jnp.where(low_half if a == 0 else jnp.logical_not(low_half), qp, jnp.zeros_like(qp))
            s = lax.dot_general(qm, kk, (((1,), (1,)), ((), ())), preferred_element_type=F32) * ATTN_SCALE
            logits = jnp.where(valid, s - _alibi_slope(h) * distf, -jnp.inf)
            sink = sinks_ref[h]
            m = jnp.maximum(jnp.max(logits, axis=-1, keepdims=True), sink)
            pexp = jnp.exp(logits - m)
            denom = jnp.sum(pexp, axis=-1, keepdims=True) + jnp.exp(sink - m)
            outs.append(jnp.dot(pexp.astype(BF16), vv, preferred_element_type=F32) / denom)
        o_ref[0, :, p * V7X_LANES:(p + 1) * V7X_LANES] = jnp.where(low_half, outs[0], outs[1]).astype(o_ref.dtype)


def _swa(proj, sinks, col0):
    b, s, _ = proj.shape
    w = SWA_WINDOW
    assert s % w == 0 and col0 % (SWA_Q_HEADS * HEAD_DIM) == 0
    qw = SWA_Q_HEADS * HEAD_DIM
    qblk = col0 // qw
    kblk = (col0 + qw) // V7X_LANES
    vblk = kblk + 1
    return pl.pallas_call(
        _swa_kernel,
        grid=(b, s // w),
        in_specs=[
            pl.BlockSpec(memory_space=pltpu.SMEM),
            pl.BlockSpec((1, w, qw), lambda bi, i: (bi, i, qblk)),
            pl.BlockSpec((1, w, V7X_LANES), lambda bi, i: (bi, jnp.maximum(i - 1, 0), kblk)),
            pl.BlockSpec((1, w, V7X_LANES), lambda bi, i: (bi, i, kblk)),
            pl.BlockSpec((1, w, V7X_LANES), lambda bi, i: (bi, jnp.maximum(i - 1, 0), vblk)),
            pl.BlockSpec((1, w, V7X_LANES), lambda bi, i: (bi, i, vblk)),
        ],
        out_specs=pl.BlockSpec((1, w, qw), lambda bi, i: (bi, i, 0)),
        out_shape=jax.ShapeDtypeStruct((b, s, qw), BF16),
        compiler_params=_params("parallel", "parallel"),
        name="swa",
    )(sinks.astype(F32), proj, proj, proj, proj, proj)


def _moba_kernel(slopes_ref, q_ref, k_ref, v_ref, o_ref, kmean_ref, lhs_ref, sb_ref, m_ref, acc_ref, *,
                 n_blocks, grp):
    pair = pl.program_id(1)
    i = pl.program_id(2)
    blk = MOBA_BLOCK
    lanes = V7X_LANES
    half = lanes // 2
    n_sel = min(MOBA_TOPK, n_blocks)

    @pl.when(i == 0)
    def _():
        kmean_ref[...] = jnp.zeros_like(kmean_ref)

    qp = q_ref[0]
    row0 = pl.multiple_of(i * blk, blk)
    k_own = k_ref[0, pl.ds(row0, blk), :]
    v_own = v_ref[0, pl.ds(row0, blk), :]
    lane_id = lax.broadcasted_iota(jnp.int32, (blk, lanes), 1)
    low_half = lane_id < HEAD_DIM
    lane_f = lane_id.astype(F32)
    r_id = lax.broadcasted_iota(jnp.int32, (blk, blk), 0)
    c_id = lax.broadcasted_iota(jnp.int32, (blk, blk), 1)
    rel = (r_id - c_id).astype(F32)
    causal = c_id <= r_id
    kmean_b = kmean_ref[...].astype(BF16)
    nt = (((1,), (1,)), ((), ()))

    def v_and_ones(vb, a):
        low = lax.broadcasted_iota(jnp.int32, vb.shape, 1) < HEAD_DIM
        return jnp.where(low if a == 0 else jnp.logical_not(low), vb, jnp.ones_like(vb))

    for a in range(2):
        slope = slopes_ref[2 * pair + a]
        qm = jnp.where(low_half if a == 0 else jnp.logical_not(low_half), qp, jnp.zeros_like(qp))
        gate = lax.dot_general(qm, kmean_b, nt, preferred_element_type=F32)
        gate = jnp.where(lane_id < i, gate, -jnp.inf)
        picked = jnp.zeros((blk, lanes), F32)
        for _ in range(n_sel):
            mx = jnp.max(gate, axis=-1, keepdims=True)
            first = jnp.min(jnp.where(gate == mx, lane_f, float(lanes)), axis=-1, keepdims=True)
            pick = (lane_f == first) & (mx > -jnp.inf)
            picked = jnp.where(pick, 1.0, picked)
            gate = jnp.where(pick, -jnp.inf, gate)
        sel = picked > 0.0
        off = (slope * float(blk)) * (lane_f - i.astype(F32))
        off_hi = _bf16_truncate(off)
        code_hi = jnp.where(sel, off_hi, MASKED_SCORE)
        code_lo = pltpu.roll(jnp.where(sel, off - off_hi, 0.0), half, 1)
        code = jnp.where(low_half, code_hi, code_lo).astype(BF16)
        lhs_ref[a] = jnp.concatenate([qm, code], axis=1)
        sb = slope * rel
        sb_ref[a] = sb
        s = lax.dot_general(qm, k_own, nt, preferred_element_type=F32)
        logits = jnp.where(causal, s - sb, -jnp.inf)
        m = jnp.max(logits, axis=-1, keepdims=True)
        p = jnp.exp2(logits - m)
        m_ref[a] = jnp.broadcast_to(m, (blk, lanes))
        acc_ref[a] = jnp.dot(p.astype(BF16), v_and_ones(v_own, a), preferred_element_type=F32)

    gk = grp * blk
    g_lane = lax.broadcasted_iota(jnp.int32, (gk, lanes), 1) % half
    g_blk = lax.broadcasted_iota(jnp.int32, (gk, lanes), 0) // blk

    def past_group(gi, carry):
        r0 = pl.multiple_of(gi * gk, gk)
        kg = k_ref[0, pl.ds(r0, gk), :]
        vg = v_ref[0, pl.ds(r0, gk), :]
        one_hot = jnp.where(g_lane == g_blk + gi * grp, 1.0, 0.0).astype(BF16)
        rhs = jnp.concatenate([kg, one_hot], axis=1)
        for a in range(2):
            s = lax.dot_general(lhs_ref[a], rhs, nt, preferred_element_type=F32)
            logits = s - jnp.concatenate([sb_ref[a]] * grp, axis=1)
            m_old = m_ref[a]
            m_new = jnp.maximum(m_old, jnp.max(logits, axis=-1, keepdims=True))
            alpha = jnp.exp2(m_old - m_new)
            p = jnp.exp2(logits - jnp.concatenate([m_new] * (gk // lanes), axis=1))
            acc_ref[a] = alpha * acc_ref[a] + jnp.dot(p.astype(BF16), v_and_ones(vg, a),
                                                      preferred_element_type=F32)
            m_ref[a] = m_new
        return carry

    lax.fori_loop(0, (i + grp - 1) // grp, past_group, 0)

    o0 = acc_ref[0] / pltpu.roll(acc_ref[0], half, 1)
    o1 = acc_ref[1] / pltpu.roll(acc_ref[1], half, 1)
    o_ref[0] = jnp.where(low_half, o0, o1).astype(o_ref.dtype)
    kmean_ref[pl.ds(i, 1), :] = jnp.mean(k_own.astype(F32), axis=0, keepdims=True)


def _moba(proj, col0):
    b, s, _ = proj.shape
    blk = MOBA_BLOCK
    assert s % blk == 0 and col0 % V7X_LANES == 0
    n_blocks = s // blk
    assert n_blocks <= V7X_LANES // 2
    grp = MOBA_GROUP if n_blocks % MOBA_GROUP == 0 else 1
    n_pairs = MOBA_HEADS // 2
    hw = MOBA_HEADS * HEAD_DIM
    qb, kb, vb = col0 // V7X_LANES, (col0 + hw) // V7X_LANES, (col0 + 2 * hw) // V7X_LANES
    slopes = jnp.asarray([_alibi_slope(SWA_Q_HEADS + h) * LOG2_E for h in range(MOBA_HEADS)], dtype=F32)
    return pl.pallas_call(
        functools.partial(_moba_kernel, n_blocks=n_blocks, grp=grp),
        grid=(b, n_pairs, n_blocks),
        in_specs=[
            pl.BlockSpec(memory_space=pltpu.SMEM),
            pl.BlockSpec((1, blk, V7X_LANES), lambda bi, p, i: (bi, i, qb + p)),
            pl.BlockSpec((1, s, V7X_LANES), lambda bi, p, i: (bi, 0, kb + p)),
            pl.BlockSpec((1, s, V7X_LANES), lambda bi, p, i: (bi, 0, vb + p)),
        ],
        out_specs=pl.BlockSpec((1, blk, V7X_LANES), lambda bi, p, i: (bi, i, p)),
        out_shape=jax.ShapeDtypeStruct((b, s, hw), BF16),
        scratch_shapes=[
            pltpu.VMEM((V7X_LANES, V7X_LANES), F32),
            pltpu.VMEM((2, blk, 2 * V7X_LANES), BF16),
            pltpu.VMEM((2, blk, blk), F32),
            pltpu.VMEM((2, blk, V7X_LANES), F32),
            pltpu.VMEM((2, blk, V7X_LANES), F32),
        ],
        compiler_params=_params("arbitrary", "arbitrary", "arbitrary"),
        name="moba",
    )(slopes, proj, proj, proj)


def _block_ref_rows(b, m):
    c, n = b.shape
    if 2 * m >= 8:
        b3 = b.reshape(c // (2 * m), 2 * m, n)
        return jnp.broadcast_to(b3[:, m - 1:m, :], b3.shape).reshape(c, n)
    r = lax.broadcasted_iota(jnp.int32, (c, n), 0)
    if m == 1:
        return jnp.where(r % 2 == 1, pltpu.roll(b, 1, 0), b)
    assert m == 2
    q4 = r % 4
    up1 = pltpu.roll(b, c - 1, 0)
    dn1 = pltpu.roll(b, 1, 0)
    dn2 = pltpu.roll(b, 2, 0)
    return jnp.where(q4 == 0, up1, jnp.where(q4 == 1, b, jnp.where(q4 == 2, dn1, dn2)))


def _hgrn_kernel(lbp_ref, ng_ref, q_ref, f_ref, i_ref, g_ref, o_ref, st_ref, lv_ref, *, layer, chunk):
    c = chunk
    ci = pl.program_id(2)
    t_id = lax.broadcasted_iota(jnp.int32, (c, c), 0)
    s_id = lax.broadcasted_iota(jnp.int32, (c, c), 1)

    @pl.when(ci == 0)
    def _():
        st_ref[...] = jnp.zeros_like(st_ref)
        x = t_id ^ s_id
        code = jnp.zeros((c, c), jnp.int32)
        bit = 1
        while bit < c:
            code = code + (x >= bit).astype(jnp.int32)
            bit *= 2
        lv_ref[...] = jnp.where(s_id <= t_id, code, -1)

    lbp = lbp_ref[...].astype(F32)
    e = jnp.exp(lbp - jnp.max(lbp, axis=0, keepdims=True))
    sm = e / jnp.sum(e, axis=0, keepdims=True)
    lb = jnp.sum(sm[1:layer + 1], axis=0, keepdims=True) if layer >= 1 else jnp.zeros((1, lbp.shape[1]), F32)

    qraw = q_ref[0].astype(F32)
    fraw = f_ref[0].astype(F32)
    v = i_ref[0].astype(F32)
    graw = g_ref[0].astype(F32)
    q = qraw * jax.nn.sigmoid(qraw)
    f_t = lb + (1.0 - lb) * jax.nn.sigmoid(fraw)
    g = jnp.log(f_t)
    kk = (1.0 - lb) * jax.nn.sigmoid(-fraw)

    tril = jnp.where(s_id <= t_id, 1.0, 0.0).astype(BF16)
    g_hi = _bf16_truncate(g)
    r1 = g - g_hi
    g_mid = _bf16_truncate(r1)
    g_lo = r1 - g_mid
    b = (jnp.dot(tril, g_hi.astype(BF16), preferred_element_type=F32)
         + jnp.dot(tril, g_mid.astype(BF16), preferred_element_type=F32)
         + jnp.dot(tril, g_lo.astype(BF16), preferred_element_type=F32))

    nt = (((1,), (1,)), ((), ()))
    st = st_ref[...]
    o = lax.dot_general((q * jnp.exp(b)).astype(BF16), st.astype(BF16), nt, preferred_element_type=F32)

    lv = lv_ref[...]
    a = jnp.where(lv == 0, lax.dot_general(q.astype(BF16), kk.astype(BF16), nt, preferred_element_type=F32), 0.0)
    m, code = 1, 1
    while m < c:
        dec = jnp.exp(-jnp.abs(b - _block_ref_rows(b, m)))
        p = lax.dot_general((q * dec).astype(BF16), (kk * dec).astype(BF16), nt, preferred_element_type=F32)
        a = jnp.where(lv == code, p, a)
        m, code = m * 2, code + 1
    o = o + jnp.dot(a.astype(BF16), v.astype(BF16), preferred_element_type=F32)

    b_last = b[c - 1:c, :]
    k_end = (kk * jnp.exp(b_last - b)).astype(BF16)
    st_ref[...] = st * jnp.exp(b_last) + jnp.dot(v.T.astype(BF16), k_end, preferred_element_type=F32)

    rms = o * lax.rsqrt(jnp.mean(o * o, axis=-1, keepdims=True) + RMS_EPS) * ng_ref[...].astype(F32)
    o_ref[0] = (rms * (graw * jax.nn.sigmoid(graw))).astype(o_ref.dtype)


def _hgrn(proj, lower_bounds, norm_g, layer):
    b, s, _ = proj.shape
    c = min(HGRN_CHUNK, s)
    assert s % c == 0 and c % 8 == 0 and (c & (c - 1)) == 0
    h = HGRN_HEADS
    depth = lower_bounds.shape[0]

    def col(j):
        return lambda bi, hi, ci: (bi, ci, hi + j * h)

    return pl.pallas_call(
        functools.partial(_hgrn_kernel, layer=layer, chunk=c),
        grid=(b, h, s // c),
        in_specs=[
            pl.BlockSpec((depth, HGRN_DK), lambda bi, hi, ci: (0, hi)),
            pl.BlockSpec((1, HGRN_DV), lambda bi, hi, ci: (0, 0)),
            pl.BlockSpec((1, c, HGRN_DK), col(0)),
            pl.BlockSpec((1, c, HGRN_DK), col(1)),
            pl.BlockSpec((1, c, HGRN_DV), col(2)),
            pl.BlockSpec((1, c, HGRN_DK), col(3)),
        ],
        out_specs=pl.BlockSpec((1, c, HGRN_DV), lambda bi, hi, ci: (bi, ci, hi)),
        out_shape=jax.ShapeDtypeStruct((b, s, h * HGRN_DV), BF16),
        scratch_shapes=[pltpu.VMEM((HGRN_DV, HGRN_DK), F32), pltpu.VMEM((c, c), jnp.int32)],
        compiler_params=_params("arbitrary", "arbitrary", "arbitrary"),
        name="hgrn",
    )(lower_bounds, norm_g.reshape(1, HGRN_DV), proj, proj, proj, proj)


def _swa_head_order():
    order = []
    for p in range(SWA_GROUP):
        for a in range(SWA_KV_HEADS):
            h = a * SWA_GROUP + p
            order.extend(range(h * HEAD_DIM, (h + 1) * HEAD_DIM))
    return jnp.asarray(order, dtype=jnp.int32)


def _attention_layer(x, batch, w_in, sinks, w_out, g, b):
    t, d = x.shape
    s = t // batch
    perm = _swa_head_order()
    qa_w = SWA_Q_HEADS * HEAD_DIM
    kv_w = 2 * SWA_KV_HEADS * HEAD_DIM
    qb_w = MOBA_HEADS * HEAD_DIM
    w_in_p = jnp.concatenate([w_in[:, :qa_w][:, perm], w_in[:, qa_w:qa_w + kv_w],
                              w_in[:, qa_w + kv_w:qa_w + kv_w + qb_w] * (ATTN_SCALE * LOG2_E),
                              w_in[:, qa_w + kv_w + qb_w:]], axis=1)
    w_out_p = jnp.concatenate([w_out[:qa_w][perm], w_out[qa_w:]], axis=0)
    proj = _proj(x, w_in_p, BF16).reshape(batch, s, -1)
    oa = _swa(proj, sinks, 0)
    ob = _moba(proj, qa_w + kv_w)
    o = jnp.concatenate([oa, ob], axis=-1).reshape(t, -1)
    return _out_proj_ln(x, o, w_out_p, g, b)


def _hgrn_layer(x, batch, w_in, norm_g, w_out, lower_bounds, layer, g, b):
    t, d = x.shape
    s = t // batch
    proj = _proj(x, w_in, F32).reshape(batch, s, -1)
    o = _hgrn(proj, lower_bounds, norm_g, layer).reshape(t, -1)
    return _out_proj_ln(x, o, w_out, g, b)


def kernel(x, ffn_w1, ffn_w3, ffn_w2, ln_g, ln_b, attn_w_in, attn_sinks, attn_w_out,
           hgrn_w_in, hgrn_norm_g, hgrn_w_out, hgrn_lower_bounds):
    batch, seq, d = x.shape
    xf = x.reshape(batch * seq, d)
    depth = ffn_w1.shape[0]
    for l in range(depth):
        j = l // 2
        xf = _ffn_ln(xf, ffn_w1[l, 0], ffn_w3[l, 0], ffn_w2[l, 0], ln_g[l, 0], ln_b[l, 0])
        if l % 2 == 0:
            xf = _attention_layer(xf, batch, attn_w_in[j], attn_sinks[j], attn_w_out[j], ln_g[l, 1], ln_b[l, 1])
        else:
            xf = _hgrn_layer(xf, batch, hgrn_w_in[j], hgrn_norm_g[j], hgrn_w_out[j], hgrn_lower_bounds, l,
                             ln_g[l, 1], ln_b[l, 1])
        xf = _ffn_ln(xf, ffn_w1[l, 1], ffn_w3[l, 1], ffn_w2[l, 1], ln_g[l, 2], ln_b[l, 2])
    return xf.reshape(batch, seq, d)
```

```python
import functools

import jax
import jax.numpy as jnp
import numpy as np
from jax import lax
from jax.experimental import pallas as pl
from jax.experimental.pallas import tpu as pltpu

F32 = jnp.float32
BF16 = jnp.bfloat16

DEPTH = 2
HEAD_DIM = 64
SWA_Q_HEADS = 8
SWA_KV_HEADS = 2
SWA_GROUP = SWA_Q_HEADS // SWA_KV_HEADS
SWA_WINDOW = 128
MOBA_HEADS = 8
MOBA_BLOCK = 256
MOBA_TOPK = 3
HGRN_HEADS = 8
HGRN_DK = 128
HGRN_DV = 128
DEEPNORM_ALPHA = (2 * DEPTH) ** 0.25
FFN_RES_WEIGHT = 0.5
LN_EPS = 1e-5
RMS_EPS = 1e-6
ATTN_SCALE = HEAD_DIM ** -0.5
LOG2_E = 1.4426950408889634
N_ATTN_HEADS = SWA_Q_HEADS + MOBA_HEADS

V7X_LANES = 128
V7X_VMEM_BYTES = 64 * 1024 * 1024
VMEM_LIMIT_BYTES = 56 * 1024 * 1024

FFN_CHUNK = 2816
FFN_TOKEN_TILE = 512
HGRN_CHUNK = 256
MOBA_GROUP = 4
MASKED_SCORE = -1e30


def _alibi_slope(i):
    return float(2.0 ** (-8.0 * (i + 1) / N_ATTN_HEADS))


def _token_tile(t, cap):
    tm = min(t, cap)
    assert t % tm == 0, (t, tm)
    return tm


def _params(*sem):
    return pltpu.CompilerParams(dimension_semantics=sem, vmem_limit_bytes=VMEM_LIMIT_BYTES)


def _const_spec(shape):
    return pl.BlockSpec(shape, lambda *_: (0,) * len(shape), pipeline_mode=pl.Buffered(1))


def _bf16_truncate(x):
    bits = lax.bitcast_convert_type(x, jnp.uint32) & jnp.uint32(0xFFFF0000)
    return lax.bitcast_convert_type(bits, F32)


def _layer_norm_rows(y, g, b):
    mu = jnp.mean(y, axis=-1, keepdims=True)
    d = y - mu
    var = jnp.mean(d * d, axis=-1, keepdims=True)
    return d * lax.rsqrt(var + LN_EPS) * g + b


def _ffn_ln_kernel(x_ref, w1_ref, w3_ref, w2_ref, g_ref, b_ref, o_ref, xb_ref, acc_ref, *, n_chunks):
    xb_ref[...] = x_ref[...].astype(BF16)

    def chunk(c):
        xb = xb_ref[...]
        h1 = jnp.dot(xb, w1_ref[c], preferred_element_type=F32)
        h3 = jnp.dot(xb, w3_ref[c], preferred_element_type=F32)
        h = (h1 * jax.nn.sigmoid(h1)) * h3
        return jnp.dot(h.astype(BF16), w2_ref[c], preferred_element_type=F32)

    acc_ref[...] = chunk(0)

    def more(c, carry):
        acc_ref[...] += chunk(c)
        return carry

    lax.fori_loop(1, n_chunks, more, 0)

    y = DEEPNORM_ALPHA * x_ref[...] + FFN_RES_WEIGHT * acc_ref[...]
    o_ref[...] = _layer_norm_rows(y, g_ref[...], b_ref[...])


def _ffn_ln(x, w1, w3, w2, g, b):
    t, d = x.shape
    d_ff = w1.shape[1]
    assert d_ff % FFN_CHUNK == 0
    n_chunks = d_ff // FFN_CHUNK
    w1c = w1.astype(BF16).reshape(d, n_chunks, FFN_CHUNK).transpose(1, 0, 2)
    w3c = w3.astype(BF16).reshape(d, n_chunks, FFN_CHUNK).transpose(1, 0, 2)
    w2c = w2.astype(BF16).reshape(n_chunks, FFN_CHUNK, d)
    tm = _token_tile(t, FFN_TOKEN_TILE)
    return pl.pallas_call(
        functools.partial(_ffn_ln_kernel, n_chunks=n_chunks),
        grid=(t // tm,),
        in_specs=[
            pl.BlockSpec((tm, d), lambda i: (i, 0)),
            _const_spec((n_chunks, d, FFN_CHUNK)),
            _const_spec((n_chunks, d, FFN_CHUNK)),
            _const_spec((n_chunks, FFN_CHUNK, d)),
            _const_spec((1, d)),
            _const_spec((1, d)),
        ],
        out_specs=pl.BlockSpec((tm, d), lambda i: (i, 0)),
        out_shape=jax.ShapeDtypeStruct((t, d), F32),
        scratch_shapes=[pltpu.VMEM((tm, d), BF16), pltpu.VMEM((tm, d), F32)],
        compiler_params=_params("parallel"),
        name="ffn_ln",
    )(x, w1c, w3c, w2c, g.reshape(1, d), b.reshape(1, d))


def _proj_kernel(x_ref, w_ref, o_ref, *, col_chunk):
    xb = x_ref[...].astype(BF16)
    n = w_ref.shape[1]
    for c0 in range(0, n, col_chunk):
        o_ref[:, c0:c0 + col_chunk] = jnp.dot(
            xb, w_ref[:, c0:c0 + col_chunk], preferred_element_type=F32).astype(o_ref.dtype)


def _proj(x, w, out_dtype):
    t, d = x.shape
    n = w.shape[1]
    col_chunk = 256
    assert n % col_chunk == 0
    tm = _token_tile(t, 512)
    return pl.pallas_call(
        functools.partial(_proj_kernel, col_chunk=col_chunk),
        grid=(t // tm,),
        in_specs=[pl.BlockSpec((tm, d), lambda i: (i, 0)), _const_spec((d, n))],
        out_specs=pl.BlockSpec((tm, n), lambda i: (i, 0)),
        out_shape=jax.ShapeDtypeStruct((t, n), out_dtype),
        compiler_params=_params("parallel"),
        name="in_proj",
    )(x, w.astype(BF16))


def _out_proj_ln_kernel(x_ref, o_ref, w_ref, g_ref, b_ref, y_ref):
    h = jnp.dot(o_ref[...], w_ref[...], preferred_element_type=F32)
    y = DEEPNORM_ALPHA * x_ref[...] + h
    y_ref[...] = _layer_norm_rows(y, g_ref[...], b_ref[...])


def _out_proj_ln(x, o, w_out, g, b):
    t, d = x.shape
    k = o.shape[1]
    tm = _token_tile(t, 512)
    return pl.pallas_call(
        _out_proj_ln_kernel,
        grid=(t // tm,),
        in_specs=[
            pl.BlockSpec((tm, d), lambda i: (i, 0)),
            pl.BlockSpec((tm, k), lambda i: (i, 0)),
            _const_spec((k, d)),
            _const_spec((1, d)),
            _const_spec((1, d)),
        ],
        out_specs=pl.BlockSpec((tm, d), lambda i: (i, 0)),
        out_shape=jax.ShapeDtypeStruct((t, d), F32),
        compiler_params=_params("parallel"),
        name="out_proj_ln",
    )(x, o, w_out.astype(BF16), g.reshape(1, d), b.reshape(1, d))


def _swa_kernel(sinks_ref, q_ref, kp_ref, kc_ref, vp_ref, vc_ref, o_ref):
    i = pl.program_id(1)
    w = SWA_WINDOW
    kk = jnp.concatenate([kp_ref[0], kc_ref[0]], axis=0)
    vv = jnp.concatenate([vp_ref[0], vc_ref[0]], axis=0)
    qi = lax.broadcasted_iota(jnp.int32, (w, 2 * w), 0)
    kj = lax.broadcasted_iota(jnp.int32, (w, 2 * w), 1)
    dist = qi + w - kj
    valid = (dist >= 0) & (dist < w) & ((i > 0) | (kj >= w))
    distf = dist.astype(F32)
    low_half = lax.broadcasted_iota(jnp.int32, (w, V7X_LANES), 1) < HEAD_DIM
    n_pairs = SWA_Q_HEADS // 2
    for p in range(n_pairs):
        qp = q_ref[0, :, p * V7X_LANES:(p + 1) * V7X_LANES]
        outs = []
        for a in range(2):
            h = a * SWA_GROUP + p
            qm = jnp.where(low_half if a == 0 else jnp.logical_not(low_half), qp, jnp.zeros_like(qp))
            s = lax.dot_general(qm, kk, (((1,), (1,)), ((), ())), preferred_element_type=F32) * ATTN_SCALE
            logits = jnp.where(valid, s - _alibi_slope(h) * distf, -jnp.inf)
            sink = sinks_ref[h]
            m = jnp.maximum(jnp.max(logits, axis=-1, keepdims=True), sink)
            pexp = jnp.exp(logits - m)
            denom = jnp.sum(pexp, axis=-1, keepdims=True) + jnp.exp(sink - m)
            outs.append(jnp.dot(pexp.astype(BF16), vv, preferred_element_type=F32) / denom)
        o_ref[0, :, p * V7X_LANES:(p + 1) * V7X_LANES] = jnp.where(low_half, outs[0], outs[1]).astype(o_ref.dtype)


def _swa(proj, sinks, col0):
    b, s, _ = proj.shape
    w = SWA_WINDOW
    assert s % w == 0 and col0 % (SWA_Q_HEADS * HEAD_DIM) == 0
    qw = SWA_Q_HEADS * HEAD_DIM
    qblk = col0 // qw
    kblk = (col0 + qw) // V7X_LANES
    vblk = kblk + 1
    return pl.pallas_call(
        _swa_kernel,
        grid=(b, s // w),
        in_specs=[
            pl.BlockSpec(memory_space=pltpu.SMEM),
            pl.BlockSpec((1, w, qw), lambda bi, i: (bi, i, qblk)),
            pl.BlockSpec((1, w, V7X_LANES), lambda bi, i: (bi, jnp.maximum(i - 1, 0), kblk)),
            pl.BlockSpec((1, w, V7X_LANES), lambda bi, i: (bi, i, kblk)),
            pl.BlockSpec((1, w, V7X_LANES), lambda bi, i: (bi, jnp.maximum(i - 1, 0), vblk)),
            pl.BlockSpec((1, w, V7X_LANES), lambda bi, i: (bi, i, vblk)),
        ],
        out_specs=pl.BlockSpec((1, w, qw), lambda bi, i: (bi, i, 0)),
        out_shape=jax.ShapeDtypeStruct((b, s, qw), BF16),
        compiler_params=_params("parallel", "parallel"),
        name="swa",
    )(sinks.astype(F32), proj, proj, proj, proj, proj)


def _split3_bf16(x):
    parts, rest = [], np.float32(x)
    for _ in range(3):
        hi = (np.asarray(rest, np.float32).view(np.uint32) & np.uint32(0xFFFF0000)).view(np.float32)
        parts.append(float(hi))
        rest = np.float32(rest - hi)
    return parts


def _moba_kernel(slopes_ref, q_ref, k_ref, v_ref, o_ref, kmean_ref, rhs_ref, va_ref, lhs_ref, m_ref, acc_ref,
                 sa_ref, sb_ref, *, n_blocks, grp):
    pair = pl.program_id(1)
    i = pl.program_id(2)
    blk = MOBA_BLOCK
    lanes = V7X_LANES
    half = lanes // 2
    n_sel = min(MOBA_TOPK, n_blocks)
    lane_id = lax.broadcasted_iota(jnp.int32, (blk, lanes), 1)
    low_half = lane_id < HEAD_DIM

    @pl.when(i == 0)
    def _():
        kmean_ref[...] = jnp.zeros_like(kmean_ref)
        pos_in_blk = lax.broadcasted_iota(jnp.int32, (blk, lanes), 0).astype(F32)
        in_blk_lanes = (lane_id >= half) & (lane_id < half + 3)
        in_pos_lanes = (lane_id >= half + 3) & (lane_id < half + 6)

        def fill(j, carry):
            r0 = pl.multiple_of(j * blk, blk)
            aug = jnp.where(lane_id == j, 1.0,
                            jnp.where(in_blk_lanes, (j * blk).astype(F32), jnp.where(in_pos_lanes, pos_in_blk, 0.0)))
            rhs_ref[pl.ds(r0, blk), :] = jnp.concatenate([k_ref[0, pl.ds(r0, blk), :], aug.astype(BF16)], axis=1)
            vb = v_ref[0, pl.ds(r0, blk), :]
            va_ref[0, pl.ds(r0, blk), :] = jnp.where(low_half, vb, jnp.ones_like(vb))
            va_ref[1, pl.ds(r0, blk), :] = jnp.where(low_half, jnp.ones_like(vb), vb)
            return carry

        lax.fori_loop(0, n_blocks, fill, 0)

    qp = q_ref[0]
    row0 = pl.multiple_of(i * blk, blk)
    rhs_own = rhs_ref[pl.ds(row0, blk), :]
    r_id = lax.broadcasted_iota(jnp.int32, (blk, blk), 0)
    c_id = lax.broadcasted_iota(jnp.int32, (blk, blk), 1)
    causal = c_id <= r_id
    kmean_b = kmean_ref[...].astype(BF16)
    nt = (((1,), (1,)), ((), ()))
    slope_part = (lane_id - half) % 3

    qms = [jnp.where(low_half, qp, jnp.zeros_like(qp)), jnp.where(low_half, jnp.zeros_like(qp), qp)]
    lane2 = lax.broadcasted_iota(jnp.int32, (2 * blk, lanes), 1)
    lane2_f = lane2.astype(F32)
    gate = lax.dot_general(jnp.concatenate(qms, axis=0), kmean_b, nt, preferred_element_type=F32)
    gate = jnp.where(lane2 < i, gate, -jnp.inf)
    picked2 = jnp.zeros((2 * blk, lanes), F32)
    for _ in range(n_sel):
        mx = jnp.max(gate, axis=-1, keepdims=True)
        first = jnp.min(jnp.where(gate == mx, lane2_f, float(lanes)), axis=-1, keepdims=True)
        pick = (lane2_f == first) & (mx > -jnp.inf)
        picked2 = jnp.where(pick, 1.0, picked2)
        gate = jnp.where(pick, -jnp.inf, gate)

    for a in range(2):
        s3 = [slopes_ref[6 * pair + 3 * a + t] for t in range(3)]
        qm = qms[a]
        picked = picked2[a * blk:(a + 1) * blk]
        slope_lanes = jnp.where(lane_id < half + 6,
                                jnp.where(slope_part == 0, s3[0], jnp.where(slope_part == 1, s3[1], s3[2])), 0.0)
        code = jnp.where(low_half, jnp.where(picked > 0.0, 0.0, MASKED_SCORE), slope_lanes)
        lhs_ref[a] = jnp.concatenate([qm, code.astype(BF16)], axis=1)
        code_own = jnp.where(lane_id == i, 0.0, code)
        lhs_own = jnp.concatenate([qm, code_own.astype(BF16)], axis=1)
        s = lax.dot_general(lhs_own, rhs_own, nt, preferred_element_type=F32)
        logits = jnp.where(causal, s, -jnp.inf)
        m = jnp.max(logits, axis=-1, keepdims=True)
        p = jnp.exp2(logits - m)
        m_ref[a] = jnp.broadcast_to(m, (blk, lanes))
        acc_ref[a] = jnp.dot(p.astype(BF16), va_ref[a, pl.ds(row0, blk), :], preferred_element_type=F32)

    gk = grp * blk
    n_groups = n_blocks // grp

    def scores(g, s_ref):
        r0 = pl.multiple_of(jnp.minimum(g, n_groups - 1) * gk, gk)
        rhs = rhs_ref[pl.ds(r0, gk), :]
        for a in range(2):
            s_ref[a] = lax.dot_general(lhs_ref[a], rhs, nt, preferred_element_type=F32)

    def absorb(g, s_ref):
        r0 = pl.multiple_of(g * gk, gk)
        for a in range(2):
            s = s_ref[a]
            m_old = m_ref[a]
            m_new = jnp.maximum(m_old, jnp.max(s, axis=-1, keepdims=True))
            alpha = jnp.exp2(m_old - m_new)
            p = jnp.exp2(s - jnp.concatenate([m_new] * (gk // lanes), axis=1))
            acc_ref[a] = alpha * acc_ref[a] + jnp.dot(p.astype(BF16), va_ref[a, pl.ds(r0, gk), :],
                                                      preferred_element_type=F32)
            m_ref[a] = m_new

    scores(0, sa_ref)

    def group_pair(h, carry):
        g = 2 * h
        scores(g + 1, sb_ref)
        absorb(g, sa_ref)
        scores(g + 2, sa_ref)
        absorb(g + 1, sb_ref)
        return carry

    lax.fori_loop(0, (i + 2 * grp - 1) // (2 * grp), group_pair, 0)

    o0 = acc_ref[0] / pltpu.roll(acc_ref[0], half, 1)
    o1 = acc_ref[1] / pltpu.roll(acc_ref[1], half, 1)
    o_ref[0] = jnp.where(low_half, o0, o1).astype(o_ref.dtype)
    k_own = k_ref[0, pl.ds(row0, blk), :]
    kmean_ref[pl.ds(i, 1), :] = jnp.mean(k_own.astype(F32), axis=0, keepdims=True)


def _moba(proj, col0):
    b, s, _ = proj.shape
    blk = MOBA_BLOCK
    assert s % blk == 0 and col0 % V7X_LANES == 0
    n_blocks = s // blk
    assert n_blocks <= V7X_LANES // 2
    grp = MOBA_GROUP
    while n_blocks % (2 * grp) != 0:
        grp //= 2
    assert grp >= 1
    n_pairs = MOBA_HEADS // 2
    hw = MOBA_HEADS * HEAD_DIM
    qb, kb, vb = col0 // V7X_LANES, (col0 + hw) // V7X_LANES, (col0 + 2 * hw) // V7X_LANES
    slopes = jnp.asarray([part for h in range(MOBA_HEADS)
                          for part in _split3_bf16(_alibi_slope(SWA_Q_HEADS + h) * LOG2_E)], dtype=F32)
    whole_seq = functools.partial(pl.BlockSpec, (1, s, V7X_LANES), pipeline_mode=pl.Buffered(1))
    return pl.pallas_call(
        functools.partial(_moba_kernel, n_blocks=n_blocks, grp=grp),
        grid=(b, n_pairs, n_blocks),
        in_specs=[
            pl.BlockSpec(memory_space=pltpu.SMEM),
            pl.BlockSpec((1, blk, V7X_LANES), lambda bi, p, i: (bi, i, qb + p)),
            whole_seq(lambda bi, p, i: (bi, 0, kb + p)),
            whole_seq(lambda bi, p, i: (bi, 0, vb + p)),
        ],
        out_specs=pl.BlockSpec((1, blk, V7X_LANES), lambda bi, p, i: (bi, i, p)),
        out_shape=jax.ShapeDtypeStruct((b, s, hw), BF16),
        scratch_shapes=[
            pltpu.VMEM((V7X_LANES, V7X_LANES), F32),
            pltpu.VMEM((s, 2 * V7X_LANES), BF16),
            pltpu.VMEM((2, s, V7X_LANES), BF16),
            pltpu.VMEM((2, blk, 2 * V7X_LANES), BF16),
            pltpu.VMEM((2, blk, V7X_LANES), F32),
            pltpu.VMEM((2, blk, V7X_LANES), F32),
            pltpu.VMEM((2, blk, grp * blk), F32),
            pltpu.VMEM((2, blk, grp * blk), F32),
        ],
        compiler_params=_params("arbitrary", "arbitrary", "arbitrary"),
        name="moba",
    )(slopes, proj, proj, proj)


def _block_ref_rows(b, m):
    c, n = b.shape
    if 2 * m >= 8:
        b3 = b.reshape(c // (2 * m), 2 * m, n)
        return jnp.broadcast_to(b3[:, m - 1:m, :], b3.shape).reshape(c, n)
    r = lax.broadcasted_iota(jnp.int32, (c, n), 0)
    if m == 1:
        return jnp.where(r % 2 == 1, pltpu.roll(b, 1, 0), b)
    assert m == 2
    q4 = r % 4
    up1 = pltpu.roll(b, c - 1, 0)
    dn1 = pltpu.roll(b, 1, 0)
    dn2 = pltpu.roll(b, 2, 0)
    return jnp.where(q4 == 0, up1, jnp.where(q4 == 1, b, jnp.where(q4 == 2, dn1, dn2)))


def _hgrn_kernel(lbp_ref, ng_ref, q_ref, f_ref, i_ref, g_ref, o_ref, st_ref, lv_ref, *, layer, chunk):
    c = chunk
    ci = pl.program_id(2)
    t_id = lax.broadcasted_iota(jnp.int32, (c, c), 0)
    s_id = lax.broadcasted_iota(jnp.int32, (c, c), 1)

    @pl.when(ci == 0)
    def _():
        st_ref[...] = jnp.zeros_like(st_ref)
        x = t_id ^ s_id
        code = jnp.zeros((c, c), jnp.int32)
        bit = 1
        while bit < c:
            code = code + (x >= bit).astype(jnp.int32)
            bit *= 2
        lv_ref[...] = jnp.where(s_id <= t_id, code, -1)

    lbp = lbp_ref[...].astype(F32)
    e = jnp.exp(lbp - jnp.max(lbp, axis=0, keepdims=True))
    sm = e / jnp.sum(e, axis=0, keepdims=True)
    lb = jnp.sum(sm[1:layer + 1], axis=0, keepdims=True) if layer >= 1 else jnp.zeros((1, lbp.shape[1]), F32)

    qraw = q_ref[0].astype(F32)
    fraw = f_ref[0].astype(F32)
    v = i_ref[0].astype(F32)
    graw = g_ref[0].astype(F32)
    q = qraw * jax.nn.sigmoid(qraw)
    f_t = lb + (1.0 - lb) * jax.nn.sigmoid(fraw)
    g = jnp.log(f_t)
    kk = (1.0 - lb) * jax.nn.sigmoid(-fraw)

    tril = jnp.where(s_id <= t_id, 1.0, 0.0).astype(BF16)
    g_hi = _bf16_truncate(g)
    r1 = g - g_hi
    g_mid = _bf16_truncate(r1)
    g_lo = r1 - g_mid
    b = (jnp.dot(tril, g_hi.astype(BF16), preferred_element_type=F32)
         + jnp.dot(tril, g_mid.astype(BF16), preferred_element_type=F32)
         + jnp.dot(tril, g_lo.astype(BF16), preferred_element_type=F32))

    nt = (((1,), (1,)), ((), ()))
    st = st_ref[...]
    o = lax.dot_general((q * jnp.exp(b)).astype(BF16), st.astype(BF16), nt, preferred_element_type=F32)

    lv = lv_ref[...]
    a = jnp.where(lv == 0, lax.dot_general(q.astype(BF16), kk.astype(BF16), nt, preferred_element_type=F32), 0.0)
    m, code = 1, 1
    while m < c:
        dec = jnp.exp(-jnp.abs(b - _block_ref_rows(b, m)))
        p = lax.dot_general((q * dec).astype(BF16), (kk * dec).astype(BF16), nt, preferred_element_type=F32)
        a = jnp.where(lv == code, p, a)
        m, code = m * 2, code + 1
    o = o + jnp.dot(a.astype(BF16), v.astype(BF16), preferred_element_type=F32)

    b_last = b[c - 1:c, :]
    k_end = (kk * jnp.exp(b_last - b)).astype(BF16)
    st_ref[...] = st * jnp.exp(b_last) + jnp.dot(v.T.astype(BF16), k_end, preferred_element_type=F32)

    rms = o * lax.rsqrt(jnp.mean(o * o, axis=-1, keepdims=True) + RMS_EPS) * ng_ref[...].astype(F32)
    o_ref[0] = (rms * (graw * jax.nn.sigmoid(graw))).astype(o_ref.dtype)


def _hgrn(proj, lower_bounds, norm_g, layer):
    b, s, _ = proj.shape
    c = min(HGRN_CHUNK, s)
    assert s % c == 0 and c % 8 == 0 and (c & (c - 1)) == 0
    h = HGRN_HEADS
    depth = lower_bounds.shape[0]

    def col(j):
        return lambda bi, hi, ci: (bi, ci, hi + j * h)

    return pl.pallas_call(
        functools.partial(_hgrn_kernel, layer=layer, chunk=c),
        grid=(b, h, s // c),
        in_specs=[
            pl.BlockSpec((depth, HGRN_DK), lambda bi, hi, ci: (0, hi)),
            pl.BlockSpec((1, HGRN_DV), lambda bi, hi, ci: (0, 0)),
            pl.BlockSpec((1, c, HGRN_DK), col(0)),
            pl.BlockSpec((1, c, HGRN_DK), col(1)),
            pl.BlockSpec((1, c, HGRN_DV), col(2)),
            pl.BlockSpec((1, c, HGRN_DK), col(3)),
        ],
        out_specs=pl.BlockSpec((1, c, HGRN_DV), lambda bi, hi, ci: (bi, ci, hi)),
        out_shape=jax.ShapeDtypeStruct((b, s, h * HGRN_DV), BF16),
        scratch_shapes=[pltpu.VMEM((HGRN_DV, HGRN_DK), F32), pltpu.VMEM((c, c), jnp.int32)],
        compiler_params=_params("arbitrary", "arbitrary", "arbitrary"),
        name="hgrn",
    )(lower_bounds, norm_g.reshape(1, HGRN_DV), proj, proj, proj, proj)


def _swa_head_order():
    order = []
    for p in range(SWA_GROUP):
        for a in range(SWA_KV_HEADS):
            h = a * SWA_GROUP + p
            order.extend(range(h * HEAD_DIM, (h + 1) * HEAD_DIM))
    return jnp.asarray(order, dtype=jnp.int32)


def _attention_layer(x, batch, w_in, sinks, w_out, g, b):
    t, d = x.shape
    s = t // batch
    perm = _swa_head_order()
    qa_w = SWA_Q_HEADS * HEAD_DIM
    kv_w = 2 * SWA_KV_HEADS * HEAD_DIM
    qb_w = MOBA_HEADS * HEAD_DIM
    w_in_p = jnp.concatenate([w_in[:, :qa_w][:, perm], w_in[:, qa_w:qa_w + kv_w],
                              w_in[:, qa_w + kv_w:qa_w + kv_w + qb_w] * (ATTN_SCALE * LOG2_E),
                              w_in[:, qa_w + kv_w + qb_w:]], axis=1)
    w_out_p = jnp.concatenate([w_out[:qa_w][perm], w_out[qa_w:]], axis=0)
    proj = _proj(x, w_in_p, BF16).reshape(batch, s, -1)
    oa = _swa(proj, sinks, 0)
    ob = _moba(proj, qa_w + kv_w)
    o = jnp.concatenate([oa, ob], axis=-1).reshape(t, -1)
    return _out_proj_ln(x, o, w_out_p, g, b)


def _hgrn_layer(x, batch, w_in, norm_g, w_out, lower_bounds, layer, g, b):
    t, d = x.shape
    s = t // batch
    proj = _proj(x, w_in, BF16).reshape(batch, s, -1)
    o = _hgrn(proj, lower_bounds, norm_g, layer).reshape(t, -1)
    return _out_proj_ln(x, o, w_out, g, b)


def kernel(x, ffn_w1, ffn_w3, ffn_w2, ln_g, ln_b, attn_w_in, attn_sinks, attn_w_out,
           hgrn_w_in, hgrn_norm_g, hgrn_w_out, hgrn_lower_bounds):
    batch, seq, d = x.shape
    xf = x.reshape(batch * seq, d)
    depth = ffn_w1.shape[0]
    for l in range(depth):
        j = l // 2
        xf = _ffn_ln(xf, ffn_w1[l, 0], ffn_w3[l, 0], ffn_w2[l, 0], ln_g[l, 0], ln_b[l, 0])
        if l % 2 == 0:
            xf = _attention_layer(xf, batch, attn_w_in[j], attn_sinks[j], attn_w_out[j], ln_g[l, 1], ln_b[l, 1])
        else:
            xf = _hgrn_layer(xf, batch, hgrn_w_in[j], hgrn_norm_g[j], hgrn_w_out[j], hgrn_lower_bounds, l,
                             ln_g[l, 1], ln_b[l, 1])
        xf = _ffn_ln(xf, ffn_w1[l, 1], ffn_w3[l, 1], ffn_w2[l, 1], ln_g[l, 2], ln_b[l, 2])
    return xf.reshape(batch, seq, d)
```

```python
import functools

import jax
import jax.numpy as jnp
import numpy as np
from jax import lax
from jax.experimental import pallas as pl
from jax.experimental.pallas import tpu as pltpu

F32 = jnp.float32
BF16 = jnp.bfloat16

DEPTH = 2
HEAD_DIM = 64
SWA_Q_HEADS = 8
SWA_KV_HEADS = 2
SWA_GROUP = SWA_Q_HEADS // SWA_KV_HEADS
SWA_WINDOW = 128
MOBA_HEADS = 8
MOBA_BLOCK = 256
MOBA_TOPK = 3
HGRN_HEADS = 8
HGRN_DK = 128
HGRN_DV = 128
DEEPNORM_ALPHA = (2 * DEPTH) ** 0.25
FFN_RES_WEIGHT = 0.5
LN_EPS = 1e-5
RMS_EPS = 1e-6
ATTN_SCALE = HEAD_DIM ** -0.5
LOG2_E = 1.4426950408889634
N_ATTN_HEADS = SWA_Q_HEADS + MOBA_HEADS

V7X_LANES = 128
V7X_VMEM_BYTES = 64 * 1024 * 1024
VMEM_LIMIT_BYTES = 56 * 1024 * 1024

FFN_CHUNK = 2816
FFN_TOKEN_TILE = 512
HGRN_CHUNK = 256
HGRN_HEADS_PER_STEP = 8
MOBA_GROUP = 4
MOBA_Q_BLOCKS = 2
MASKED_SCORE = -1e30


def _alibi_slope(i):
    return float(2.0 ** (-8.0 * (i + 1) / N_ATTN_HEADS))


def _token_tile(t, cap):
    tm = min(t, cap)
    assert t % tm == 0, (t, tm)
    return tm


def _params(*sem):
    return pltpu.CompilerParams(dimension_semantics=sem, vmem_limit_bytes=VMEM_LIMIT_BYTES)


def _const_spec(shape):
    return pl.BlockSpec(shape, lambda *_: (0,) * len(shape), pipeline_mode=pl.Buffered(1))


def _bf16_truncate(x):
    bits = lax.bitcast_convert_type(x, jnp.uint32) & jnp.uint32(0xFFFF0000)
    return lax.bitcast_convert_type(bits, F32)


def _layer_norm_rows(y, g, b):
    mu = jnp.mean(y, axis=-1, keepdims=True)
    d = y - mu
    var = jnp.mean(d * d, axis=-1, keepdims=True)
    return d * lax.rsqrt(var + LN_EPS) * g + b


def _ffn_ln_kernel(x_ref, w1_ref, w3_ref, w2_ref, g_ref, b_ref, o_ref, xb_ref, acc_ref, *, n_chunks):
    xb_ref[...] = x_ref[...].astype(BF16)

    def chunk(c):
        xb = xb_ref[...]
        h1 = jnp.dot(xb, w1_ref[c], preferred_element_type=F32)
        h3 = jnp.dot(xb, w3_ref[c], preferred_element_type=F32)
        h = (h1 * jax.nn.sigmoid(h1)) * h3
        return jnp.dot(h.astype(BF16), w2_ref[c], preferred_element_type=F32)

    acc_ref[...] = chunk(0)

    def more(c, carry):
        acc_ref[...] += chunk(c)
        return carry

    lax.fori_loop(1, n_chunks, more, 0)

    y = DEEPNORM_ALPHA * x_ref[...] + FFN_RES_WEIGHT * acc_ref[...]
    o_ref[...] = _layer_norm_rows(y, g_ref[...], b_ref[...])


def _ffn_ln(x, w1, w3, w2, g, b):
    t, d = x.shape
    d_ff = w1.shape[1]
    assert d_ff % FFN_CHUNK == 0
    n_chunks = d_ff // FFN_CHUNK
    w1c = w1.astype(BF16).reshape(d, n_chunks, FFN_CHUNK).transpose(1, 0, 2)
    w3c = w3.astype(BF16).reshape(d, n_chunks, FFN_CHUNK).transpose(1, 0, 2)
    w2c = w2.astype(BF16).reshape(n_chunks, FFN_CHUNK, d)
    tm = _token_tile(t, FFN_TOKEN_TILE)
    return pl.pallas_call(
        functools.partial(_ffn_ln_kernel, n_chunks=n_chunks),
        grid=(t // tm,),
        in_specs=[
            pl.BlockSpec((tm, d), lambda i: (i, 0)),
            _const_spec((n_chunks, d, FFN_CHUNK)),
            _const_spec((n_chunks, d, FFN_CHUNK)),
            _const_spec((n_chunks, FFN_CHUNK, d)),
            _const_spec((1, d)),
            _const_spec((1, d)),
        ],
        out_specs=pl.BlockSpec((tm, d), lambda i: (i, 0)),
        out_shape=jax.ShapeDtypeStruct((t, d), F32),
        scratch_shapes=[pltpu.VMEM((tm, d), BF16), pltpu.VMEM((tm, d), F32)],
        compiler_params=_params("parallel"),
        name="ffn_ln",
    )(x, w1c, w3c, w2c, g.reshape(1, d), b.reshape(1, d))


def _proj_kernel(x_ref, w_ref, o_ref, *, col_chunk):
    xb = x_ref[...].astype(BF16)
    n = w_ref.shape[1]
    for c0 in range(0, n, col_chunk):
        o_ref[:, c0:c0 + col_chunk] = jnp.dot(
            xb, w_ref[:, c0:c0 + col_chunk], preferred_element_type=F32).astype(o_ref.dtype)


def _proj(x, w, out_dtype):
    t, d = x.shape
    n = w.shape[1]
    col_chunk = 256
    assert n % col_chunk == 0
    tm = _token_tile(t, 512)
    return pl.pallas_call(
        functools.partial(_proj_kernel, col_chunk=col_chunk),
        grid=(t // tm,),
        in_specs=[pl.BlockSpec((tm, d), lambda i: (i, 0)), _const_spec((d, n))],
        out_specs=pl.BlockSpec((tm, n), lambda i: (i, 0)),
        out_shape=jax.ShapeDtypeStruct((t, n), out_dtype),
        compiler_params=_params("parallel"),
        name="in_proj",
    )(x, w.astype(BF16))


def _out_proj_ln_kernel(x_ref, o_ref, w_ref, g_ref, b_ref, y_ref):
    h = jnp.dot(o_ref[...], w_ref[...], preferred_element_type=F32)
    y = DEEPNORM_ALPHA * x_ref[...] + h
    y_ref[...] = _layer_norm_rows(y, g_ref[...], b_ref[...])


def _out_proj_ln(x, o, w_out, g, b):
    t, d = x.shape
    k = o.shape[1]
    tm = _token_tile(t, 512)
    return pl.pallas_call(
        _out_proj_ln_kernel,
        grid=(t // tm,),
        in_specs=[
            pl.BlockSpec((tm, d), lambda i: (i, 0)),
            pl.BlockSpec((tm, k), lambda i: (i, 0)),
            _const_spec((k, d)),
            _const_spec((1, d)),
            _const_spec((1, d)),
        ],
        out_specs=pl.BlockSpec((tm, d), lambda i: (i, 0)),
        out_shape=jax.ShapeDtypeStruct((t, d), F32),
        compiler_params=_params("parallel"),
        name="out_proj_ln",
    )(x, o, w_out.astype(BF16), g.reshape(1, d), b.reshape(1, d))


def _swa_kernel(sinks_ref, q_ref, kp_ref, kc_ref, vp_ref, vc_ref, o_ref):
    i = pl.program_id(1)
    w = SWA_WINDOW
    kk = jnp.concatenate([kp_ref[0], kc_ref[0]], axis=0)
    vv = jnp.concatenate([vp_ref[0], vc_ref[0]], axis=0)
    qi = lax.broadcasted_iota(jnp.int32, (w, 2 * w), 0)
    kj = lax.broadcasted_iota(jnp.int32, (w, 2 * w), 1)
    dist = qi + w - kj
    valid = (dist >= 0) & (dist < w) & ((i > 0) | (kj >= w))
    distf = dist.astype(F32)
    low_half = lax.broadcasted_iota(jnp.int32, (w, V7X_LANES), 1) < HEAD_DIM
    n_pairs = SWA_Q_HEADS // 2
    tiles = []
    for p in range(n_pairs):
        qp = q_ref[0, :, p * V7X_LANES:(p + 1) * V7X_LANES]
        tiles += [jnp.where(low_half, qp, jnp.zeros_like(qp)), jnp.where(low_half, jnp.zeros_like(qp), qp)]
    s_all = lax.dot_general(jnp.concatenate(tiles, axis=0), kk, (((1,), (1,)), ((), ())),
                            preferred_element_type=F32)
    probs, denoms = [], []
    for j in range(SWA_Q_HEADS):
        p, a = divmod(j, 2)
        h = a * SWA_GROUP + p
        s = s_all[j * w:(j + 1) * w] * ATTN_SCALE
        logits = jnp.where(valid, s - _alibi_slope(h) * distf, -jnp.inf)
        sink = sinks_ref[h]
        m = jnp.maximum(jnp.max(logits, axis=-1, keepdims=True), sink)
        pexp = jnp.exp(logits - m)
        denoms.append(jnp.sum(pexp, axis=-1, keepdims=True) + jnp.exp(sink - m))
        probs.append(pexp.astype(BF16))
    o_all = jnp.dot(jnp.concatenate(probs, axis=0), vv, preferred_element_type=F32)
    for p in range(n_pairs):
        o0 = o_all[(2 * p) * w:(2 * p + 1) * w] / denoms[2 * p]
        o1 = o_all[(2 * p + 1) * w:(2 * p + 2) * w] / denoms[2 * p + 1]
        o_ref[0, :, p * V7X_LANES:(p + 1) * V7X_LANES] = jnp.where(low_half, o0, o1).astype(o_ref.dtype)


def _swa(proj, sinks, col0):
    b, s, _ = proj.shape
    w = SWA_WINDOW
    assert s % w == 0 and col0 % (SWA_Q_HEADS * HEAD_DIM) == 0
    qw = SWA_Q_HEADS * HEAD_DIM
    qblk = col0 // qw
    kblk = (col0 + qw) // V7X_LANES
    vblk = kblk + 1
    return pl.pallas_call(
        _swa_kernel,
        grid=(b, s // w),
        in_specs=[
            pl.BlockSpec(memory_space=pltpu.SMEM),
            pl.BlockSpec((1, w, qw), lambda bi, i: (bi, i, qblk)),
            pl.BlockSpec((1, w, V7X_LANES), lambda bi, i: (bi, jnp.maximum(i - 1, 0), kblk)),
            pl.BlockSpec((1, w, V7X_LANES), lambda bi, i: (bi, i, kblk)),
            pl.BlockSpec((1, w, V7X_LANES), lambda bi, i: (bi, jnp.maximum(i - 1, 0), vblk)),
            pl.BlockSpec((1, w, V7X_LANES), lambda bi, i: (bi, i, vblk)),
        ],
        out_specs=pl.BlockSpec((1, w, qw), lambda bi, i: (bi, i, 0)),
        out_shape=jax.ShapeDtypeStruct((b, s, qw), BF16),
        compiler_params=_params("parallel", "parallel"),
        name="swa",
    )(sinks.astype(F32), proj, proj, proj, proj, proj)


def _split3_bf16(x):
    parts, rest = [], np.float32(x)
    for _ in range(3):
        hi = (np.asarray(rest, np.float32).view(np.uint32) & np.uint32(0xFFFF0000)).view(np.float32)
        parts.append(float(hi))
        rest = np.float32(rest - hi)
    return parts


def _moba_kernel(slopes_ref, q_ref, k_ref, v_ref, o_ref, kmean_ref, rhs_ref, va_ref, lhs_ref, m_ref, acc_ref,
                 sa_ref, sb_ref, *, n_blocks, grp, qb):
    pair = pl.program_id(1)
    ti = pl.program_id(2)
    blk = MOBA_BLOCK
    lanes = V7X_LANES
    half = lanes // 2
    qt = qb * blk
    i0 = ti * qb
    n_sel = min(MOBA_TOPK, n_blocks)

    @pl.when(ti == 0)
    def _():
        kmean_ref[...] = jnp.zeros_like(kmean_ref)
        lane_b = lax.broadcasted_iota(jnp.int32, (blk, lanes), 1)
        low_b = lane_b < HEAD_DIM
        pos_in_blk = lax.broadcasted_iota(jnp.int32, (blk, lanes), 0).astype(F32)
        in_blk_lanes = (lane_b >= half) & (lane_b < half + 3)
        in_pos_lanes = (lane_b >= half + 3) & (lane_b < half + 6)

        def fill(j, carry):
            r0 = pl.multiple_of(j * blk, blk)
            blk_start = lax.convert_element_type(j * blk, F32)
            aug = jnp.where(lane_b == j, 1.0,
                            jnp.where(in_blk_lanes, blk_start, jnp.where(in_pos_lanes, pos_in_blk, 0.0)))
            rhs_ref[pl.ds(r0, blk), :] = jnp.concatenate([k_ref[0, pl.ds(r0, blk), :], aug.astype(BF16)], axis=1)
            vb = v_ref[0, pl.ds(r0, blk), :]
            va_ref[0, pl.ds(r0, blk), :] = jnp.where(low_b, vb, jnp.ones_like(vb))
            va_ref[1, pl.ds(r0, blk), :] = jnp.where(low_b, jnp.ones_like(vb), vb)
            return carry

        lax.fori_loop(0, n_blocks, fill, 0)

    row0 = pl.multiple_of(ti * qt, qt)
    for hb in range(qb):
        k_blk = k_ref[0, pl.ds(row0 + hb * blk, blk), :]
        kmean_ref[pl.ds(i0 + hb, 1), :] = jnp.mean(k_blk.astype(F32), axis=0, keepdims=True)
    kmean_b = kmean_ref[...].astype(BF16)

    lane_id = lax.broadcasted_iota(jnp.int32, (qt, lanes), 1)
    low_half = lane_id < HEAD_DIM
    own_blk = i0 + lax.broadcasted_iota(jnp.int32, (qt, lanes), 0) // blk
    qp = q_ref[0]
    rhs_own = rhs_ref[pl.ds(row0, qt), :]
    r_id = lax.broadcasted_iota(jnp.int32, (qt, qt), 0)
    c_id = lax.broadcasted_iota(jnp.int32, (qt, qt), 1)
    causal = c_id <= r_id
    nt = (((1,), (1,)), ((), ()))
    slope_part = (lane_id - half) % 3

    qms = [jnp.where(low_half, qp, jnp.zeros_like(qp)), jnp.where(low_half, jnp.zeros_like(qp), qp)]
    lane2 = lax.broadcasted_iota(jnp.int32, (2 * qt, lanes), 1)
    lane2_f = lane2.astype(F32)
    own2 = jnp.concatenate([own_blk, own_blk], axis=0)
    gate = lax.dot_general(jnp.concatenate(qms, axis=0), kmean_b, nt, preferred_element_type=F32)
    gate = jnp.where(lane2 < own2, gate, -jnp.inf)
    picked2 = jnp.zeros((2 * qt, lanes), F32)
    for _ in range(n_sel):
        mx = jnp.max(gate, axis=-1, keepdims=True)
        first = jnp.min(jnp.where(gate == mx, lane2_f, float(lanes)), axis=-1, keepdims=True)
        pick = (lane2_f == first) & (mx > -jnp.inf)
        picked2 = jnp.where(pick, 1.0, picked2)
        gate = jnp.where(pick, -jnp.inf, gate)

    for a in range(2):
        s3 = [slopes_ref[6 * pair + 3 * a + t] for t in range(3)]
        qm = qms[a]
        sel_code = jnp.where(picked2[a * qt:(a + 1) * qt] > 0.0, 0.0, MASKED_SCORE)
        slope_lanes = jnp.where(lane_id < half + 6,
                                jnp.where(slope_part == 0, s3[0], jnp.where(slope_part == 1, s3[1], s3[2])), 0.0)
        walk_code = jnp.where(low_half, jnp.where(lane_id >= i0, MASKED_SCORE, sel_code), slope_lanes)
        lhs_ref[a] = jnp.concatenate([qm, walk_code.astype(BF16)], axis=1)
        own_code = jnp.where(low_half, jnp.where(lane_id == own_blk, 0.0, sel_code), slope_lanes)
        lhs_own = jnp.concatenate([qm, own_code.astype(BF16)], axis=1)
        s = lax.dot_general(lhs_own, rhs_own, nt, preferred_element_type=F32)
        logits = jnp.where(causal, s, -jnp.inf)
        m = jnp.max(logits, axis=-1, keepdims=True)
        p = jnp.exp2(logits - m)
        m_ref[a] = jnp.broadcast_to(m, (qt, lanes))
        acc_ref[a] = jnp.dot(p.astype(BF16), va_ref[a, pl.ds(row0, qt), :], preferred_element_type=F32)

    gk = grp * blk
    n_groups = n_blocks // grp

    def scores(g, s_ref):
        r0 = pl.multiple_of(jnp.minimum(g, n_groups - 1) * gk, gk)
        rhs = rhs_ref[pl.ds(r0, gk), :]
        for a in range(2):
            s_ref[a] = lax.dot_general(lhs_ref[a], rhs, nt, preferred_element_type=F32)

    def absorb(g, s_ref):
        r0 = pl.multiple_of(g * gk, gk)
        for a in range(2):
            s = s_ref[a]
            m_old = m_ref[a]
            m_new = jnp.maximum(m_old, jnp.max(s, axis=-1, keepdims=True))
            alpha = jnp.exp2(m_old - m_new)
            p = jnp.exp2(s - jnp.concatenate([m_new] * (gk // lanes), axis=1))
            acc_ref[a] = alpha * acc_ref[a] + jnp.dot(p.astype(BF16), va_ref[a, pl.ds(r0, gk), :],
                                                      preferred_element_type=F32)
            m_ref[a] = m_new

    scores(0, sa_ref)

    def group_pair(h, carry):
        g = 2 * h
        scores(g + 1, sb_ref)
        absorb(g, sa_ref)
        scores(g + 2, sa_ref)
        absorb(g + 1, sb_ref)
        return carry

    lax.fori_loop(0, (i0 + 2 * grp - 1) // (2 * grp), group_pair, 0)

    o0 = acc_ref[0] / pltpu.roll(acc_ref[0], half, 1)
    o1 = acc_ref[1] / pltpu.roll(acc_ref[1], half, 1)
    o_ref[0] = jnp.where(low_half, o0, o1).astype(o_ref.dtype)


def _moba(proj, col0):
    b, s, _ = proj.shape
    blk = MOBA_BLOCK
    assert s % blk == 0 and col0 % V7X_LANES == 0
    n_blocks = s // blk
    assert n_blocks <= V7X_LANES // 2
    grp = MOBA_GROUP
    while n_blocks % (2 * grp) != 0:
        grp //= 2
    assert grp >= 1
    qb = MOBA_Q_BLOCKS if n_blocks % MOBA_Q_BLOCKS == 0 else 1
    qt = qb * blk
    n_pairs = MOBA_HEADS // 2
    hw = MOBA_HEADS * HEAD_DIM
    qc, kc, vc = col0 // V7X_LANES, (col0 + hw) // V7X_LANES, (col0 + 2 * hw) // V7X_LANES
    slopes = jnp.asarray([part for h in range(MOBA_HEADS)
                          for part in _split3_bf16(_alibi_slope(SWA_Q_HEADS + h) * LOG2_E)], dtype=F32)
    whole_seq = functools.partial(pl.BlockSpec, (1, s, V7X_LANES), pipeline_mode=pl.Buffered(1))
    return pl.pallas_call(
        functools.partial(_moba_kernel, n_blocks=n_blocks, grp=grp, qb=qb),
        grid=(b, n_pairs, n_blocks // qb),
        in_specs=[
            pl.BlockSpec(memory_space=pltpu.SMEM),
            pl.BlockSpec((1, qt, V7X_LANES), lambda bi, p, ti: (bi, ti, qc + p)),
            whole_seq(lambda bi, p, ti: (bi, 0, kc + p)),
            whole_seq(lambda bi, p, ti: (bi, 0, vc + p)),
        ],
        out_specs=pl.BlockSpec((1, qt, V7X_LANES), lambda bi, p, ti: (bi, ti, p)),
        out_shape=jax.ShapeDtypeStruct((b, s, hw), BF16),
        scratch_shapes=[
            pltpu.VMEM((V7X_LANES, V7X_LANES), F32),
            pltpu.VMEM((s, 2 * V7X_LANES), BF16),
            pltpu.VMEM((2, s, V7X_LANES), BF16),
            pltpu.VMEM((2, qt, 2 * V7X_LANES), BF16),
            pltpu.VMEM((2, qt, V7X_LANES), F32),
            pltpu.VMEM((2, qt, V7X_LANES), F32),
            pltpu.VMEM((2, qt, grp * blk), F32),
            pltpu.VMEM((2, qt, grp * blk), F32),
        ],
        compiler_params=_params("arbitrary", "arbitrary", "arbitrary"),
        name="moba",
    )(slopes, proj, proj, proj)


def _block_ref_rows(b, m):
    c, n = b.shape
    if 2 * m >= 8:
        b3 = b.reshape(c // (2 * m), 2 * m, n)
        return jnp.broadcast_to(b3[:, m - 1:m, :], b3.shape).reshape(c, n)
    r = lax.broadcasted_iota(jnp.int32, (c, n), 0)
    if m == 1:
        return jnp.where(r % 2 == 1, pltpu.roll(b, 1, 0), b)
    assert m == 2
    q4 = r % 4
    up1 = pltpu.roll(b, c - 1, 0)
    dn1 = pltpu.roll(b, 1, 0)
    dn2 = pltpu.roll(b, 2, 0)
    return jnp.where(q4 == 0, up1, jnp.where(q4 == 1, b, jnp.where(q4 == 2, dn1, dn2)))


def _neg_abs(x):
    bits = lax.bitcast_convert_type(x, jnp.uint32) | jnp.uint32(0x80000000)
    return lax.bitcast_convert_type(bits, F32)


def _hgrn_kernel(lbp_ref, ng_ref, q_ref, f_ref, i_ref, g_ref, o_ref, st_ref, lv_ref, *, layer, chunk):
    c = chunk
    lanes = V7X_LANES
    ci = pl.program_id(2)
    t_id = lax.broadcasted_iota(jnp.int32, (c, c), 0)
    s_id = lax.broadcasted_iota(jnp.int32, (c, c), 1)

    @pl.when(ci == 0)
    def _():
        st_ref[...] = jnp.zeros_like(st_ref)
        x = t_id ^ s_id
        code = jnp.zeros((c, c), jnp.int32)
        bit = 1
        while bit < c:
            code = code + (x >= bit).astype(jnp.int32)
            bit *= 2
        lv_ref[...] = jnp.where(s_id <= t_id, code, -1)

    tril = jnp.where(s_id <= t_id, 1.0, 0.0).astype(BF16)
    for hd in range(q_ref.shape[2] // lanes):
        cols = slice(hd * lanes, (hd + 1) * lanes)
        o_ref[0, :, cols] = _hgrn_head(lbp_ref[:, cols], ng_ref[...], q_ref[0, :, cols], f_ref[0, :, cols],
                                       i_ref[0, :, cols], g_ref[0, :, cols], st_ref.at[hd], lv_ref, tril,
                                       layer=layer, c=c).astype(o_ref.dtype)


def _hgrn_head(lbp, ng, qraw, fraw, v, graw, st_ref, lv_ref, tril, *, layer, c):
    lanes = V7X_LANES
    lbp = lbp.astype(F32)
    e = jnp.exp(lbp - jnp.max(lbp, axis=0, keepdims=True))
    sm = e / jnp.sum(e, axis=0, keepdims=True)
    lb = jnp.sum(sm[1:layer + 1], axis=0, keepdims=True) if layer >= 1 else jnp.zeros((1, lbp.shape[1]), F32)

    qraw = qraw.astype(F32)
    fraw = fraw.astype(F32)
    v = v.astype(F32)
    graw = graw.astype(F32)
    q = qraw * jax.nn.sigmoid(qraw)
    f_t = lb + (1.0 - lb) * jax.nn.sigmoid(fraw)
    g = jnp.log2(f_t)
    kk = (1.0 - lb) * jax.nn.sigmoid(-fraw)

    g_hi = _bf16_truncate(g)
    r1 = g - g_hi
    g_mid = _bf16_truncate(r1)
    g_lo = r1 - g_mid
    b = (jnp.dot(tril, g_hi.astype(BF16), preferred_element_type=F32)
         + jnp.dot(tril, g_mid.astype(BF16), preferred_element_type=F32)
         + jnp.dot(tril, g_lo.astype(BF16), preferred_element_type=F32))

    nt = (((1,), (1,)), ((), ()))
    st = st_ref[...]
    o = lax.dot_general((q * jnp.exp2(b)).astype(BF16), st.astype(BF16), nt, preferred_element_type=F32)

    prods = {0: lax.dot_general(q.astype(BF16), kk.astype(BF16), nt, preferred_element_type=F32)}
    m, code = 1, 1
    while m < c:
        dec = jnp.exp2(_neg_abs(b - _block_ref_rows(b, m)))
        qe, ke = (q * dec).astype(BF16), (kk * dec).astype(BF16)
        if 2 * m == c and m % lanes == 0:
            prods[code] = lax.dot_general(qe[m:], ke[:m], nt, preferred_element_type=F32)
        else:
            prods[code] = lax.dot_general(qe, ke, nt, preferred_element_type=F32)
        m, code = m * 2, code + 1
    top = code - 1 if (c // 2) % lanes == 0 else None
    bands = []
    for r0 in range(0, c, 8):
        d0 = (r0 // lanes) * lanes
        lv = lv_ref[r0:r0 + 8, d0:d0 + lanes]
        tile = jnp.zeros((8, lanes), F32)
        for cd, p in prods.items():
            if cd == top:
                continue
            mm = 1 << (cd - 1) if cd else 0
            if cd and (r0 // mm) % 2 == 0 and mm >= 8:
                continue
            tile = jnp.where(lv == cd, p[r0:r0 + 8, d0:d0 + lanes], tile)
        cols = []
        for c0 in range(0, c, lanes):
            if c0 == d0:
                cols.append(tile)
            elif top is not None and c0 < d0:
                cols.append(prods[top][r0 - c // 2:r0 - c // 2 + 8, c0:c0 + lanes])
            else:
                cols.append(jnp.zeros((8, lanes), F32))
        bands.append(jnp.concatenate(cols, axis=1))
    a = jnp.concatenate(bands, axis=0)
    o = o + jnp.dot(a.astype(BF16), v.astype(BF16), preferred_element_type=F32)

    b_last = b[c - 1:c, :]
    k_end = (kk * jnp.exp2(b_last - b)).astype(BF16)
    st_ref[...] = st * jnp.exp2(b_last) + jnp.dot(v.T.astype(BF16), k_end, preferred_element_type=F32)

    rms = o * lax.rsqrt(jnp.mean(o * o, axis=-1, keepdims=True) + RMS_EPS) * ng.astype(F32)
    return rms * (graw * jax.nn.sigmoid(graw))


def _hgrn(proj, lower_bounds, norm_g, layer):
    b, s, _ = proj.shape
    c = min(HGRN_CHUNK, s)
    assert s % c == 0 and c % 8 == 0 and (c & (c - 1)) == 0
    hps = HGRN_HEADS_PER_STEP
    assert HGRN_HEADS % hps == 0 and HGRN_DK == HGRN_DV == V7X_LANES
    h = HGRN_HEADS // hps
    w = hps * HGRN_DK
    depth = lower_bounds.shape[0]

    def col(j):
        return lambda bi, hi, ci: (bi, ci, hi + j * h)

    return pl.pallas_call(
        functools.partial(_hgrn_kernel, layer=layer, chunk=c),
        grid=(b, h, s // c),
        in_specs=[
            pl.BlockSpec((depth, w), lambda bi, hi, ci: (0, hi)),
            pl.BlockSpec((1, HGRN_DV), lambda bi, hi, ci: (0, 0)),
            pl.BlockSpec((1, c, w), col(0)),
            pl.BlockSpec((1, c, w), col(1)),
            pl.BlockSpec((1, c, w), col(2)),
            pl.BlockSpec((1, c, w), col(3)),
        ],
        out_specs=pl.BlockSpec((1, c, w), lambda bi, hi, ci: (bi, ci, hi)),
        out_shape=jax.ShapeDtypeStruct((b, s, HGRN_HEADS * HGRN_DV), BF16),
        scratch_shapes=[pltpu.VMEM((hps, HGRN_DV, HGRN_DK), F32), pltpu.VMEM((c, c), jnp.int32)],
        compiler_params=_params("arbitrary", "arbitrary", "arbitrary"),
        name="hgrn",
    )(lower_bounds, norm_g.reshape(1, HGRN_DV), proj, proj, proj, proj)


def _swa_head_order():
    order = []
    for p in range(SWA_GROUP):
        for a in range(SWA_KV_HEADS):
            h = a * SWA_GROUP + p
            order.extend(range(h * HEAD_DIM, (h + 1) * HEAD_DIM))
    return jnp.asarray(order, dtype=jnp.int32)


def _attention_layer(x, batch, w_in, sinks, w_out, g, b):
    t, d = x.shape
    s = t // batch
    perm = _swa_head_order()
    qa_w = SWA_Q_HEADS * HEAD_DIM
    kv_w = 2 * SWA_KV_HEADS * HEAD_DIM
    qb_w = MOBA_HEADS * HEAD_DIM
    w_in_p = jnp.concatenate([w_in[:, :qa_w][:, perm], w_in[:, qa_w:qa_w + kv_w],
                              w_in[:, qa_w + kv_w:qa_w + kv_w + qb_w] * (ATTN_SCALE * LOG2_E),
                              w_in[:, qa_w + kv_w + qb_w:]], axis=1)
    w_out_p = jnp.concatenate([w_out[:qa_w][perm], w_out[qa_w:]], axis=0)
    proj = _proj(x, w_in_p, BF16).reshape(batch, s, -1)
    oa = _swa(proj, sinks, 0)
    ob = _moba(proj, qa_w + kv_w)
    o = jnp.concatenate([oa, ob], axis=-1).reshape(t, -1)
    return _out_proj_ln(x, o, w_out_p, g, b)


def _hgrn_layer(x, batch, w_in, norm_g, w_out, lower_bounds, layer, g, b):
    t, d = x.shape
    s = t // batch
    proj = _proj(x, w_in, BF16).reshape(batch, s, -1)
    o = _hgrn(proj, lower_bounds, norm_g, layer).reshape(t, -1)
    return _out_proj_ln(x, o, w_out, g, b)


def kernel(x, ffn_w1, ffn_w3, ffn_w2, ln_g, ln_b, attn_w_in, attn_sinks, attn_w_out,
           hgrn_w_in, hgrn_norm_g, hgrn_w_out, hgrn_lower_bounds):
    batch, seq, d = x.shape
    xf = x.reshape(batch * seq, d)
    depth = ffn_w1.shape[0]
    for l in range(depth):
        j = l // 2
        xf = _ffn_ln(xf, ffn_w1[l, 0], ffn_w3[l, 0], ffn_w2[l, 0], ln_g[l, 0], ln_b[l, 0])
        if l % 2 == 0:
            xf = _attention_layer(xf, batch, attn_w_in[j], attn_sinks[j], attn_w_out[j], ln_g[l, 1], ln_b[l, 1])
        else:
            xf = _hgrn_layer(xf, batch, hgrn_w_in[j], hgrn_norm_g[j], hgrn_w_out[j], hgrn_lower_bounds, l,
                             ln_g[l, 1], ln_b[l, 1])
        xf = _ffn_ln(xf, ffn_w1[l, 1], ffn_w3[l, 1], ffn_w2[l, 1], ln_g[l, 2], ln_b[l, 2])
    return xf.reshape(batch, seq, d)
```

```python
import functools

import jax
import jax.numpy as jnp
import numpy as np
from jax import lax
from jax.experimental import pallas as pl
from jax.experimental.pallas import tpu as pltpu

F32 = jnp.float32
BF16 = jnp.bfloat16

DEPTH = 2
HEAD_DIM = 64
SWA_Q_HEADS = 8
SWA_KV_HEADS = 2
SWA_GROUP = SWA_Q_HEADS // SWA_KV_HEADS
SWA_WINDOW = 128
MOBA_HEADS = 8
MOBA_BLOCK = 256
MOBA_TOPK = 3
HGRN_HEADS = 8
HGRN_DK = 128
HGRN_DV = 128
DEEPNORM_ALPHA = (2 * DEPTH) ** 0.25
FFN_RES_WEIGHT = 0.5
LN_EPS = 1e-5
RMS_EPS = 1e-6
ATTN_SCALE = HEAD_DIM ** -0.5
LOG2_E = 1.4426950408889634
N_ATTN_HEADS = SWA_Q_HEADS + MOBA_HEADS

V7X_LANES = 128
V7X_VMEM_BYTES = 64 * 1024 * 1024
VMEM_LIMIT_BYTES = 56 * 1024 * 1024

FFN_CHUNK = 2816
FFN_TOKEN_TILE = 512
HGRN_CHUNK = 256
MOBA_GROUP = 4
MOBA_Q_BLOCKS = 2
MASKED_SCORE = -1e30


def _alibi_slope(i):
    return float(2.0 ** (-8.0 * (i + 1) / N_ATTN_HEADS))


def _token_tile(t, cap):
    tm = min(t, cap)
    assert t % tm == 0, (t, tm)
    return tm


def _params(*sem):
    return pltpu.CompilerParams(dimension_semantics=sem, vmem_limit_bytes=VMEM_LIMIT_BYTES)


def _const_spec(shape):
    return pl.BlockSpec(shape, lambda *_: (0,) * len(shape), pipeline_mode=pl.Buffered(1))


def _bf16_truncate(x):
    bits = lax.bitcast_convert_type(x, jnp.uint32) & jnp.uint32(0xFFFF0000)
    return lax.bitcast_convert_type(bits, F32)


def _layer_norm_rows(y, g, b):
    mu = jnp.mean(y, axis=-1, keepdims=True)
    d = y - mu
    var = jnp.mean(d * d, axis=-1, keepdims=True)
    return d * lax.rsqrt(var + LN_EPS) * g + b


def _swiglu_ln(x_ref, w1_ref, w3_ref, w2_ref, g_ref, b_ref, o_ref, xb_ref, acc_ref, n_chunks):
    xb_ref[...] = x_ref[...].astype(BF16)

    def chunk(c):
        xb = xb_ref[...]
        h1 = jnp.dot(xb, w1_ref[c], preferred_element_type=F32)
        h3 = jnp.dot(xb, w3_ref[c], preferred_element_type=F32)
        h = (h1 * jax.nn.sigmoid(h1)) * h3
        return jnp.dot(h.astype(BF16), w2_ref[c], preferred_element_type=F32)

    acc_ref[...] = chunk(0)

    def more(c, carry):
        acc_ref[...] += chunk(c)
        return carry

    lax.fori_loop(1, n_chunks, more, 0)

    y = DEEPNORM_ALPHA * x_ref[...] + FFN_RES_WEIGHT * acc_ref[...]
    o_ref[...] = _layer_norm_rows(y, g_ref[...], b_ref[...])


def _ffn_ln_kernel(x_ref, w1_ref, w3_ref, w2_ref, g_ref, b_ref, o_ref, xb_ref, acc_ref, *, n_chunks):
    _swiglu_ln(x_ref, w1_ref, w3_ref, w2_ref, g_ref, b_ref, o_ref, xb_ref, acc_ref, n_chunks)


def _mix_ffn_ln_kernel(x_ref, oa_ref, ob_ref, wa_ref, wb_ref, gm_ref, bm_ref, w1_ref, w3_ref, w2_ref, g_ref, b_ref,
                       o_ref, xm_ref, xb_ref, acc_ref, *, n_chunks):
    h = (jnp.dot(oa_ref[...], wa_ref[...], preferred_element_type=F32)
         + jnp.dot(ob_ref[...], wb_ref[...], preferred_element_type=F32))
    xm_ref[...] = _layer_norm_rows(DEEPNORM_ALPHA * x_ref[...] + h, gm_ref[...], bm_ref[...])
    _swiglu_ln(xm_ref, w1_ref, w3_ref, w2_ref, g_ref, b_ref, o_ref, xb_ref, acc_ref, n_chunks)


def _ffn_ln(x, w1, w3, w2, g, b, mix=None):
    t, d = x.shape
    d_ff = w1.shape[1]
    assert d_ff % FFN_CHUNK == 0
    n_chunks = d_ff // FFN_CHUNK
    w1c = w1.astype(BF16).reshape(d, n_chunks, FFN_CHUNK).transpose(1, 0, 2)
    w3c = w3.astype(BF16).reshape(d, n_chunks, FFN_CHUNK).transpose(1, 0, 2)
    w2c = w2.astype(BF16).reshape(n_chunks, FFN_CHUNK, d)
    tm = _token_tile(t, FFN_TOKEN_TILE)
    row_tile = lambda width: pl.BlockSpec((tm, width), lambda i: (i, 0))
    ffn_specs = [
        _const_spec((n_chunks, d, FFN_CHUNK)),
        _const_spec((n_chunks, d, FFN_CHUNK)),
        _const_spec((n_chunks, FFN_CHUNK, d)),
        _const_spec((1, d)),
        _const_spec((1, d)),
    ]
    ffn_args = (w1c, w3c, w2c, g.reshape(1, d), b.reshape(1, d))
    scratch = [pltpu.VMEM((tm, d), BF16), pltpu.VMEM((tm, d), F32)]
    if mix is None:
        body, in_specs, args = _ffn_ln_kernel, [row_tile(d)] + ffn_specs, (x,) + ffn_args
    else:
        oa, ob, wa, wb, gm, bm = mix
        ka, kb = oa.shape[1], ob.shape[1]
        body = _mix_ffn_ln_kernel
        in_specs = [row_tile(d), row_tile(ka), row_tile(kb), _const_spec((ka, d)), _const_spec((kb, d)),
                    _const_spec((1, d)), _const_spec((1, d))] + ffn_specs
        args = (x, oa, ob, wa.astype(BF16), wb.astype(BF16), gm.reshape(1, d), bm.reshape(1, d)) + ffn_args
        scratch = [pltpu.VMEM((tm, d), F32)] + scratch
    return pl.pallas_call(
        functools.partial(body, n_chunks=n_chunks),
        grid=(t // tm,),
        in_specs=in_specs,
        out_specs=row_tile(d),
        out_shape=jax.ShapeDtypeStruct((t, d), F32),
        scratch_shapes=scratch,
        compiler_params=_params("parallel"),
        name="ffn_ln" if mix is None else "mix_ffn_ln",
    )(*args)


def _proj_kernel(x_ref, w_ref, o_ref, *, col_chunk):
    xb = x_ref[...].astype(BF16)
    n = w_ref.shape[1]
    for c0 in range(0, n, col_chunk):
        o_ref[:, c0:c0 + col_chunk] = jnp.dot(
            xb, w_ref[:, c0:c0 + col_chunk], preferred_element_type=F32).astype(o_ref.dtype)


def _proj(x, w, out_dtype):
    t, d = x.shape
    n = w.shape[1]
    col_chunk = 256
    assert n % col_chunk == 0
    tm = _token_tile(t, 512)
    return pl.pallas_call(
        functools.partial(_proj_kernel, col_chunk=col_chunk),
        grid=(t // tm,),
        in_specs=[pl.BlockSpec((tm, d), lambda i: (i, 0)), _const_spec((d, n))],
        out_specs=pl.BlockSpec((tm, n), lambda i: (i, 0)),
        out_shape=jax.ShapeDtypeStruct((t, n), out_dtype),
        compiler_params=_params("parallel"),
        name="in_proj",
    )(x, w.astype(BF16))


def _swa_kernel(sinks_ref, q_ref, kp_ref, kc_ref, vp_ref, vc_ref, o_ref):
    i = pl.program_id(1)
    w = SWA_WINDOW
    kk = jnp.concatenate([kp_ref[0], kc_ref[0]], axis=0)
    vv = jnp.concatenate([vp_ref[0], vc_ref[0]], axis=0)
    qi = lax.broadcasted_iota(jnp.int32, (w, 2 * w), 0)
    kj = lax.broadcasted_iota(jnp.int32, (w, 2 * w), 1)
    dist = qi + w - kj
    valid = (dist >= 0) & (dist < w) & ((i > 0) | (kj >= w))
    distf = dist.astype(F32)
    low_half = lax.broadcasted_iota(jnp.int32, (w, V7X_LANES), 1) < HEAD_DIM
    n_pairs = SWA_Q_HEADS // 2
    tiles = []
    for p in range(n_pairs):
        qp = q_ref[0, :, p * V7X_LANES:(p + 1) * V7X_LANES]
        tiles += [jnp.where(low_half, qp, jnp.zeros_like(qp)), jnp.where(low_half, jnp.zeros_like(qp), qp)]
    s_all = lax.dot_general(jnp.concatenate(tiles, axis=0), kk, (((1,), (1,)), ((), ())),
                            preferred_element_type=F32)
    probs, denoms = [], []
    for j in range(SWA_Q_HEADS):
        p, a = divmod(j, 2)
        h = a * SWA_GROUP + p
        s = s_all[j * w:(j + 1) * w] * ATTN_SCALE
        logits = jnp.where(valid, s - _alibi_slope(h) * distf, -jnp.inf)
        sink = sinks_ref[h]
        m = jnp.maximum(jnp.max(logits, axis=-1, keepdims=True), sink)
        pexp = jnp.exp(logits - m)
        denoms.append(jnp.sum(pexp, axis=-1, keepdims=True) + jnp.exp(sink - m))
        probs.append(pexp.astype(BF16))
    o_all = jnp.dot(jnp.concatenate(probs, axis=0), vv, preferred_element_type=F32)
    for p in range(n_pairs):
        o0 = o_all[(2 * p) * w:(2 * p + 1) * w] / denoms[2 * p]
        o1 = o_all[(2 * p + 1) * w:(2 * p + 2) * w] / denoms[2 * p + 1]
        o_ref[0, :, p * V7X_LANES:(p + 1) * V7X_LANES] = jnp.where(low_half, o0, o1).astype(o_ref.dtype)


def _swa(proj, sinks, col0):
    b, s, _ = proj.shape
    w = SWA_WINDOW
    assert s % w == 0 and col0 % (SWA_Q_HEADS * HEAD_DIM) == 0
    qw = SWA_Q_HEADS * HEAD_DIM
    qblk = col0 // qw
    kblk = (col0 + qw) // V7X_LANES
    vblk = kblk + 1
    return pl.pallas_call(
        _swa_kernel,
        grid=(b, s // w),
        in_specs=[
            pl.BlockSpec(memory_space=pltpu.SMEM),
            pl.BlockSpec((1, w, qw), lambda bi, i: (bi, i, qblk)),
            pl.BlockSpec((1, w, V7X_LANES), lambda bi, i: (bi, jnp.maximum(i - 1, 0), kblk)),
            pl.BlockSpec((1, w, V7X_LANES), lambda bi, i: (bi, i, kblk)),
            pl.BlockSpec((1, w, V7X_LANES), lambda bi, i: (bi, jnp.maximum(i - 1, 0), vblk)),
            pl.BlockSpec((1, w, V7X_LANES), lambda bi, i: (bi, i, vblk)),
        ],
        out_specs=pl.BlockSpec((1, w, qw), lambda bi, i: (bi, i, 0)),
        out_shape=jax.ShapeDtypeStruct((b, s, qw), BF16),
        compiler_params=_params("parallel", "parallel"),
        name="swa",
    )(sinks.astype(F32), proj, proj, proj, proj, proj)


def _split3_bf16(x):
    parts, rest = [], np.float32(x)
    for _ in range(3):
        hi = (np.asarray(rest, np.float32).view(np.uint32) & np.uint32(0xFFFF0000)).view(np.float32)
        parts.append(float(hi))
        rest = np.float32(rest - hi)
    return parts


def _moba_kernel(slopes_ref, q_ref, k_ref, v_ref, o_ref, kmean_ref, rhs_ref, va_ref, lhs_ref, m_ref, acc_ref,
                 sa_ref, sb_ref, *, n_blocks, grp, qb):
    pair = pl.program_id(1)
    ti = pl.program_id(2)
    blk = MOBA_BLOCK
    lanes = V7X_LANES
    half = lanes // 2
    qt = qb * blk
    i0 = ti * qb
    n_sel = min(MOBA_TOPK, n_blocks)

    @pl.when(ti == 0)
    def _():
        kmean_ref[...] = jnp.zeros_like(kmean_ref)
        lane_b = lax.broadcasted_iota(jnp.int32, (blk, lanes), 1)
        low_b = lane_b < HEAD_DIM
        pos_in_blk = lax.broadcasted_iota(jnp.int32, (blk, lanes), 0).astype(F32)
        in_blk_lanes = (lane_b >= half) & (lane_b < half + 3)
        in_pos_lanes = (lane_b >= half + 3) & (lane_b < half + 6)

        def fill(j, carry):
            r0 = pl.multiple_of(j * blk, blk)
            blk_start = lax.convert_element_type(j * blk, F32)
            aug = jnp.where(lane_b == j, 1.0,
                            jnp.where(in_blk_lanes, blk_start, jnp.where(in_pos_lanes, pos_in_blk, 0.0)))
            rhs_ref[pl.ds(r0, blk), :] = jnp.concatenate([k_ref[0, pl.ds(r0, blk), :], aug.astype(BF16)], axis=1)
            vb = v_ref[0, pl.ds(r0, blk), :]
            va_ref[0, pl.ds(r0, blk), :] = jnp.where(low_b, vb, jnp.ones_like(vb))
            va_ref[1, pl.ds(r0, blk), :] = jnp.where(low_b, jnp.ones_like(vb), vb)
            return carry

        lax.fori_loop(0, n_blocks, fill, 0)

    row0 = pl.multiple_of(ti * qt, qt)
    for hb in range(qb):
        k_blk = k_ref[0, pl.ds(row0 + hb * blk, blk), :]
        kmean_ref[pl.ds(i0 + hb, 1), :] = jnp.mean(k_blk.astype(F32), axis=0, keepdims=True)
    kmean_b = kmean_ref[...].astype(BF16)

    lane_id = lax.broadcasted_iota(jnp.int32, (qt, lanes), 1)
    low_half = lane_id < HEAD_DIM
    own_blk = i0 + lax.broadcasted_iota(jnp.int32, (qt, lanes), 0) // blk
    qp = q_ref[0]
    rhs_own = rhs_ref[pl.ds(row0, qt), :]
    r_id = lax.broadcasted_iota(jnp.int32, (qt, qt), 0)
    c_id = lax.broadcasted_iota(jnp.int32, (qt, qt), 1)
    causal = c_id <= r_id
    nt = (((1,), (1,)), ((), ()))
    slope_part = (lane_id - half) % 3

    qms = [jnp.where(low_half, qp, jnp.zeros_like(qp)), jnp.where(low_half, jnp.zeros_like(qp), qp)]
    blk_id = lax.broadcasted_iota(jnp.int32, (half, 2 * qt), 0)
    blk_f = blk_id.astype(F32)
    own_t = i0 + (lax.broadcasted_iota(jnp.int32, (half, 2 * qt), 1) % qt) // blk
    gate = lax.dot_general(kmean_b[:half], jnp.concatenate(qms, axis=0), nt, preferred_element_type=F32)
    gate = jnp.where(blk_id < own_t, gate, -jnp.inf)
    picked_t = jnp.zeros((half, 2 * qt), F32)
    for _ in range(n_sel):
        mx = jnp.max(gate, axis=0, keepdims=True)
        first = jnp.min(jnp.where(gate == mx, blk_f, float(half)), axis=0, keepdims=True)
        pick = (blk_f == first) & (mx > -jnp.inf)
        picked_t = jnp.where(pick, 1.0, picked_t)
        gate = jnp.where(pick, -jnp.inf, gate)
    picked2 = jnp.concatenate([picked_t, jnp.zeros_like(picked_t)], axis=0).T

    for a in range(2):
        s3 = [slopes_ref[6 * pair + 3 * a + t] for t in range(3)]
        qm = qms[a]
        sel_code = jnp.where(picked2[a * qt:(a + 1) * qt] > 0.0, 0.0, MASKED_SCORE)
        slope_lanes = jnp.where(lane_id < half + 6,
                                jnp.where(slope_part == 0, s3[0], jnp.where(slope_part == 1, s3[1], s3[2])), 0.0)
        walk_code = jnp.where(low_half, jnp.where(lane_id >= i0, MASKED_SCORE, sel_code), slope_lanes)
        lhs_ref[a] = jnp.concatenate([qm, walk_code.astype(BF16)], axis=1)
        own_code = jnp.where(low_half, jnp.where(lane_id == own_blk, 0.0, sel_code), slope_lanes)
        lhs_own = jnp.concatenate([qm, own_code.astype(BF16)], axis=1)
        s = lax.dot_general(lhs_own, rhs_own, nt, preferred_element_type=F32)
        logits = jnp.where(causal, s, -jnp.inf)
        m = jnp.max(logits, axis=-1, keepdims=True)
        p = jnp.exp2(logits - m)
        m_ref[a] = jnp.broadcast_to(m, (qt, lanes))
        acc_ref[a] = jnp.dot(p.astype(BF16), va_ref[a, pl.ds(row0, qt), :], preferred_element_type=F32)

    gk = grp * blk
    n_groups = n_blocks // grp

    def scores(g, s_ref):
        r0 = pl.multiple_of(jnp.minimum(g, n_groups - 1) * gk, gk)
        rhs = rhs_ref[pl.ds(r0, gk), :]
        for a in range(2):
            s_ref[a] = lax.dot_general(lhs_ref[a], rhs, nt, preferred_element_type=F32)

    def absorb(g, s_ref):
        r0 = pl.multiple_of(g * gk, gk)
        for a in range(2):
            s = s_ref[a]
            m_old = m_ref[a]
            m_new = jnp.maximum(m_old, jnp.max(s, axis=-1, keepdims=True))
            alpha = jnp.exp2(m_old - m_new)
            p = jnp.exp2(s - jnp.concatenate([m_new] * (gk // lanes), axis=1))
            acc_ref[a] = alpha * acc_ref[a] + jnp.dot(p.astype(BF16), va_ref[a, pl.ds(r0, gk), :],
                                                      preferred_element_type=F32)
            m_ref[a] = m_new

    scores(0, sa_ref)

    def group_pair(h, carry):
        g = 2 * h
        scores(g + 1, sb_ref)
        absorb(g, sa_ref)
        scores(g + 2, sa_ref)
        absorb(g + 1, sb_ref)
        return carry

    lax.fori_loop(0, (i0 + 2 * grp - 1) // (2 * grp), group_pair, 0)

    o0 = acc_ref[0] / pltpu.roll(acc_ref[0], half, 1)
    o1 = acc_ref[1] / pltpu.roll(acc_ref[1], half, 1)
    o_ref[0] = jnp.where(low_half, o0, o1).astype(o_ref.dtype)


def _moba(proj, col0):
    b, s, _ = proj.shape
    blk = MOBA_BLOCK
    assert s % blk == 0 and col0 % V7X_LANES == 0
    n_blocks = s // blk
    assert n_blocks <= V7X_LANES // 2
    grp = MOBA_GROUP
    while n_blocks % (2 * grp) != 0:
        grp //= 2
    assert grp >= 1
    qb = MOBA_Q_BLOCKS if n_blocks % MOBA_Q_BLOCKS == 0 else 1
    qt = qb * blk
    n_pairs = MOBA_HEADS // 2
    hw = MOBA_HEADS * HEAD_DIM
    qc, kc, vc = col0 // V7X_LANES, (col0 + hw) // V7X_LANES, (col0 + 2 * hw) // V7X_LANES
    slopes = jnp.asarray([part for h in range(MOBA_HEADS)
                          for part in _split3_bf16(_alibi_slope(SWA_Q_HEADS + h) * LOG2_E)], dtype=F32)
    whole_seq = functools.partial(pl.BlockSpec, (1, s, V7X_LANES), pipeline_mode=pl.Buffered(1))
    return pl.pallas_call(
        functools.partial(_moba_kernel, n_blocks=n_blocks, grp=grp, qb=qb),
        grid=(b, n_pairs, n_blocks // qb),
        in_specs=[
            pl.BlockSpec(memory_space=pltpu.SMEM),
            pl.BlockSpec((1, qt, V7X_LANES), lambda bi, p, ti: (bi, ti, qc + p)),
            whole_seq(lambda bi, p, ti: (bi, 0, kc + p)),
            whole_seq(lambda bi, p, ti: (bi, 0, vc + p)),
        ],
        out_specs=pl.BlockSpec((1, qt, V7X_LANES), lambda bi, p, ti: (bi, ti, p)),
        out_shape=jax.ShapeDtypeStruct((b, s, hw), BF16),
        scratch_shapes=[
            pltpu.VMEM((V7X_LANES, V7X_LANES), F32),
            pltpu.VMEM((s, 2 * V7X_LANES), BF16),
            pltpu.VMEM((2, s, V7X_LANES), BF16),
            pltpu.VMEM((2, qt, 2 * V7X_LANES), BF16),
            pltpu.VMEM((2, qt, V7X_LANES), F32),
            pltpu.VMEM((2, qt, V7X_LANES), F32),
            pltpu.VMEM((2, qt, grp * blk), F32),
            pltpu.VMEM((2, qt, grp * blk), F32),
        ],
        compiler_params=_params("arbitrary", "arbitrary", "arbitrary"),
        name="moba",
    )(slopes, proj, proj, proj)


def _block_ref_rows(b, m):
    c, n = b.shape
    if 2 * m >= 8:
        b3 = b.reshape(c // (2 * m), 2 * m, n)
        return jnp.broadcast_to(b3[:, m - 1:m, :], b3.shape).reshape(c, n)
    r = lax.broadcasted_iota(jnp.int32, (c, n), 0)
    if m == 1:
        return jnp.where(r % 2 == 1, pltpu.roll(b, 1, 0), b)
    assert m == 2
    q4 = r % 4
    up1 = pltpu.roll(b, c - 1, 0)
    dn1 = pltpu.roll(b, 1, 0)
    dn2 = pltpu.roll(b, 2, 0)
    return jnp.where(q4 == 0, up1, jnp.where(q4 == 1, b, jnp.where(q4 == 2, dn1, dn2)))


def _neg_abs(x):
    bits = lax.bitcast_convert_type(x, jnp.uint32) | jnp.uint32(0x80000000)
    return lax.bitcast_convert_type(bits, F32)


def _hgrn_layer_kernel(lbp_ref, ng_ref, x_ref, xn_ref, w_in_ref, w_out_ref, g_ref, b_ref, y_ref,
                       st_ref, lv_ref, proj_ref, o_ref, *, layer, chunk):
    c = chunk
    lanes = V7X_LANES
    ci = pl.program_id(1)
    t_id = lax.broadcasted_iota(jnp.int32, (c, c), 0)
    s_id = lax.broadcasted_iota(jnp.int32, (c, c), 1)

    def project(x_blk, slot):
        proj_ref[slot] = jnp.dot(x_blk.astype(BF16), w_in_ref[...], preferred_element_type=F32).astype(BF16)

    @pl.when(ci == 0)
    def _():
        st_ref[...] = jnp.zeros_like(st_ref)
        x = t_id ^ s_id
        code = jnp.zeros((c, c), jnp.int32)
        bit = 1
        while bit < c:
            code = code + (x >= bit).astype(jnp.int32)
            bit *= 2
        lv_ref[...] = jnp.where(s_id <= t_id, code, -1)
        project(x_ref[0], 0)

    slot = ci % 2
    project(xn_ref[0], 1 - slot)

    tril = jnp.where(s_id <= t_id, 1.0, 0.0).astype(BF16)
    hw = HGRN_HEADS * lanes
    for hd in range(HGRN_HEADS):
        cols = slice(hd * lanes, (hd + 1) * lanes)
        q, f, v, g = (proj_ref[slot, :, j * hw + hd * lanes:j * hw + (hd + 1) * lanes] for j in range(4))
        o_ref[:, cols] = _hgrn_head(lbp_ref[:, cols], ng_ref[...], q, f, v, g, st_ref.at[hd], lv_ref, tril,
                                    layer=layer, c=c).astype(o_ref.dtype)
    h = jnp.dot(o_ref[...], w_out_ref[...], preferred_element_type=F32)
    y = DEEPNORM_ALPHA * x_ref[0] + h
    y_ref[0] = _layer_norm_rows(y, g_ref[...], b_ref[...])


def _hgrn_head(lbp, ng, qraw, fraw, v, graw, st_ref, lv_ref, tril, *, layer, c):
    lanes = V7X_LANES
    lbp = lbp.astype(F32)
    e = jnp.exp(lbp - jnp.max(lbp, axis=0, keepdims=True))
    sm = e / jnp.sum(e, axis=0, keepdims=True)
    lb = jnp.sum(sm[1:layer + 1], axis=0, keepdims=True) if layer >= 1 else jnp.zeros((1, lbp.shape[1]), F32)

    qraw = qraw.astype(F32)
    fraw = fraw.astype(F32)
    v = v.astype(F32)
    graw = graw.astype(F32)
    q = qraw * jax.nn.sigmoid(qraw)
    f_t = lb + (1.0 - lb) * jax.nn.sigmoid(fraw)
    g = jnp.log2(f_t)
    kk = (1.0 - lb) * jax.nn.sigmoid(-fraw)

    g_hi = _bf16_truncate(g)
    r1 = g - g_hi
    g_mid = _bf16_truncate(r1)
    g_lo = r1 - g_mid
    b = (jnp.dot(tril, g_hi.astype(BF16), preferred_element_type=F32)
         + jnp.dot(tril, g_mid.astype(BF16), preferred_element_type=F32)
         + jnp.dot(tril, g_lo.astype(BF16), preferred_element_type=F32))

    nt = (((1,), (1,)), ((), ()))
    st = st_ref[...]
    o = lax.dot_general((q * jnp.exp2(b)).astype(BF16), st.astype(BF16), nt, preferred_element_type=F32)

    prods = {0: lax.dot_general(q.astype(BF16), kk.astype(BF16), nt, preferred_element_type=F32)}
    m, code = 1, 1
    while m < c:
        dec = jnp.exp2(_neg_abs(b - _block_ref_rows(b, m)))
        qe, ke = (q * dec).astype(BF16), (kk * dec).astype(BF16)
        if 2 * m == c and m % lanes == 0:
            prods[code] = lax.dot_general(qe[m:], ke[:m], nt, preferred_element_type=F32)
        else:
            prods[code] = lax.dot_general(qe, ke, nt, preferred_element_type=F32)
        m, code = m * 2, code + 1
    top = code - 1 if (c // 2) % lanes == 0 else None
    bands = []
    for r0 in range(0, c, 8):
        d0 = (r0 // lanes) * lanes
        lv = lv_ref[r0:r0 + 8, d0:d0 + lanes]
        tile = jnp.zeros((8, lanes), F32)
        for cd, p in prods.items():
            if cd == top:
                continue
            mm = 1 << (cd - 1) if cd else 0
            if cd and (r0 // mm) % 2 == 0 and mm >= 8:
                continue
            tile = jnp.where(lv == cd, p[r0:r0 + 8, d0:d0 + lanes], tile)
        cols = []
        for c0 in range(0, c, lanes):
            if c0 == d0:
                cols.append(tile)
            elif top is not None and c0 < d0:
                cols.append(prods[top][r0 - c // 2:r0 - c // 2 + 8, c0:c0 + lanes])
            else:
                cols.append(jnp.zeros((8, lanes), F32))
        bands.append(jnp.concatenate(cols, axis=1))
    a = jnp.concatenate(bands, axis=0)
    o = o + jnp.dot(a.astype(BF16), v.astype(BF16), preferred_element_type=F32)

    b_last = b[c - 1:c, :]
    k_end = (kk * jnp.exp2(b_last - b)).astype(BF16)
    st_ref[...] = st * jnp.exp2(b_last) + jnp.dot(v.T.astype(BF16), k_end, preferred_element_type=F32)

    rms = o * lax.rsqrt(jnp.mean(o * o, axis=-1, keepdims=True) + RMS_EPS) * ng.astype(F32)
    return rms * (graw * jax.nn.sigmoid(graw))


def _swa_head_order():
    order = []
    for p in range(SWA_GROUP):
        for a in range(SWA_KV_HEADS):
            h = a * SWA_GROUP + p
            order.extend(range(h * HEAD_DIM, (h + 1) * HEAD_DIM))
    return jnp.asarray(order, dtype=jnp.int32)


def _attention_mixer(x, batch, w_in, sinks, w_out):
    t, d = x.shape
    s = t // batch
    perm = _swa_head_order()
    qa_w = SWA_Q_HEADS * HEAD_DIM
    kv_w = 2 * SWA_KV_HEADS * HEAD_DIM
    qb_w = MOBA_HEADS * HEAD_DIM
    w_in_p = jnp.concatenate([w_in[:, :qa_w][:, perm], w_in[:, qa_w:qa_w + kv_w],
                              w_in[:, qa_w + kv_w:qa_w + kv_w + qb_w] * (ATTN_SCALE * LOG2_E),
                              w_in[:, qa_w + kv_w + qb_w:]], axis=1)
    proj = _proj(x, w_in_p, BF16).reshape(batch, s, -1)
    oa = _swa(proj, sinks, 0).reshape(t, qa_w)
    ob = _moba(proj, qa_w + kv_w).reshape(t, qb_w)
    return oa, ob, w_out[:qa_w][perm], w_out[qa_w:]


def _hgrn_layer(x, batch, w_in, norm_g, w_out, lower_bounds, layer, g, b):
    t, d = x.shape
    s = t // batch
    c = min(HGRN_CHUNK, s)
    assert s % c == 0 and c % 8 == 0 and (c & (c - 1)) == 0
    assert HGRN_DK == HGRN_DV == V7X_LANES
    nc = s // c
    hw = HGRN_HEADS * HGRN_DK
    depth = lower_bounds.shape[0]
    y = pl.pallas_call(
        functools.partial(_hgrn_layer_kernel, layer=layer, chunk=c),
        grid=(batch, nc),
        in_specs=[
            pl.BlockSpec((depth, hw), lambda bi, ci: (0, 0)),
            pl.BlockSpec((1, HGRN_DV), lambda bi, ci: (0, 0)),
            pl.BlockSpec((1, c, d), lambda bi, ci: (bi, ci, 0)),
            pl.BlockSpec((1, c, d), lambda bi, ci: (bi, jnp.minimum(ci + 1, nc - 1), 0)),
            _const_spec((d, 4 * hw)),
            _const_spec((hw, d)),
            _const_spec((1, d)),
            _const_spec((1, d)),
        ],
        out_specs=pl.BlockSpec((1, c, d), lambda bi, ci: (bi, ci, 0)),
        out_shape=jax.ShapeDtypeStruct((batch, s, d), F32),
        scratch_shapes=[
            pltpu.VMEM((HGRN_HEADS, HGRN_DV, HGRN_DK), F32),
            pltpu.VMEM((c, c), jnp.int32),
            pltpu.VMEM((2, c, 4 * hw), BF16),
            pltpu.VMEM((c, hw), BF16),
        ],
        compiler_params=_params("arbitrary", "arbitrary"),
        name="hgrn_layer",
    )(lower_bounds, norm_g.reshape(1, HGRN_DV), x.reshape(batch, s, d), x.reshape(batch, s, d),
      w_in.astype(BF16), w_out.astype(BF16), g.reshape(1, d), b.reshape(1, d))
    return y.reshape(t, d)


def kernel(x, ffn_w1, ffn_w3, ffn_w2, ln_g, ln_b, attn_w_in, attn_sinks, attn_w_out,
           hgrn_w_in, hgrn_norm_g, hgrn_w_out, hgrn_lower_bounds):
    batch, seq, d = x.shape
    xf = x.reshape(batch * seq, d)
    depth = ffn_w1.shape[0]
    for l in range(depth):
        j = l // 2
        xf = _ffn_ln(xf, ffn_w1[l, 0], ffn_w3[l, 0], ffn_w2[l, 0], ln_g[l, 0], ln_b[l, 0])
        if l % 2 == 0:
            mix = _attention_mixer(xf, batch, attn_w_in[j], attn_sinks[j], attn_w_out[j]) + (ln_g[l, 1], ln_b[l, 1])
        else:
            xf = _hgrn_layer(xf, batch, hgrn_w_in[j], hgrn_norm_g[j], hgrn_w_out[j], hgrn_lower_bounds, l,
                             ln_g[l, 1], ln_b[l, 1])
            mix = None
        xf = _ffn_ln(xf, ffn_w1[l, 1], ffn_w3[l, 1], ffn_w2[l, 1], ln_g[l, 2], ln_b[l, 2], mix=mix)
    return xf.reshape(batch, seq, d)
```

```python
import functools

import jax
import jax.numpy as jnp
import numpy as np
from jax import lax
from jax.experimental import pallas as pl
from jax.experimental.pallas import tpu as pltpu

F32 = jnp.float32
BF16 = jnp.bfloat16

DEPTH = 2
HEAD_DIM = 64
SWA_Q_HEADS = 8
SWA_KV_HEADS = 2
SWA_GROUP = SWA_Q_HEADS // SWA_KV_HEADS
SWA_WINDOW = 128
MOBA_HEADS = 8
MOBA_BLOCK = 256
MOBA_TOPK = 3
HGRN_HEADS = 8
HGRN_DK = 128
HGRN_DV = 128
DEEPNORM_ALPHA = (2 * DEPTH) ** 0.25
FFN_RES_WEIGHT = 0.5
LN_EPS = 1e-5
RMS_EPS = 1e-6
ATTN_SCALE = HEAD_DIM ** -0.5
LOG2_E = 1.4426950408889634
N_ATTN_HEADS = SWA_Q_HEADS + MOBA_HEADS

V7X_LANES = 128
V7X_VMEM_BYTES = 64 * 1024 * 1024
VMEM_LIMIT_BYTES = 56 * 1024 * 1024

FFN_CHUNK = 2816
FFN_TOKEN_TILE = 512
HGRN_CHUNK = 256
MOBA_GROUP = 4
MOBA_Q_BLOCKS = 2
MASKED_SCORE = -1e30


def _alibi_slope(i):
    return float(2.0 ** (-8.0 * (i + 1) / N_ATTN_HEADS))


def _token_tile(t, cap):
    tm = min(t, cap)
    assert t % tm == 0, (t, tm)
    return tm


def _params(*sem):
    return pltpu.CompilerParams(dimension_semantics=sem, vmem_limit_bytes=VMEM_LIMIT_BYTES)


def _const_spec(shape):
    return pl.BlockSpec(shape, lambda *_: (0,) * len(shape), pipeline_mode=pl.Buffered(1))


def _bf16_truncate(x):
    bits = lax.bitcast_convert_type(x, jnp.uint32) & jnp.uint32(0xFFFF0000)
    return lax.bitcast_convert_type(bits, F32)


def _layer_norm_rows(y, g, b):
    mu = jnp.mean(y, axis=-1, keepdims=True)
    d = y - mu
    var = jnp.mean(d * d, axis=-1, keepdims=True)
    return d * lax.rsqrt(var + LN_EPS) * g + b


def _swiglu_ln(x_ref, w1_ref, w3_ref, w2_ref, g_ref, b_ref, o_ref, xb_ref, acc_ref, n_chunks):
    xb_ref[...] = x_ref[...].astype(BF16)

    def chunk(c):
        xb = xb_ref[...]
        h1 = jnp.dot(xb, w1_ref[c], preferred_element_type=F32)
        h3 = jnp.dot(xb, w3_ref[c], preferred_element_type=F32)
        h = (h1 * jax.nn.sigmoid(h1)) * h3
        return jnp.dot(h.astype(BF16), w2_ref[c], preferred_element_type=F32)

    acc_ref[...] = chunk(0)

    def more(c, carry):
        acc_ref[...] += chunk(c)
        return carry

    lax.fori_loop(1, n_chunks, more, 0)

    y = DEEPNORM_ALPHA * x_ref[...] + FFN_RES_WEIGHT * acc_ref[...]
    o_ref[...] = _layer_norm_rows(y, g_ref[...], b_ref[...])


def _ffn_ln_kernel(x_ref, w1_ref, w3_ref, w2_ref, g_ref, b_ref, o_ref, xb_ref, acc_ref, *, n_chunks):
    _swiglu_ln(x_ref, w1_ref, w3_ref, w2_ref, g_ref, b_ref, o_ref, xb_ref, acc_ref, n_chunks)


def _mix_ffn_ln_kernel(x_ref, oa_ref, ob_ref, wa_ref, wb_ref, gm_ref, bm_ref, w1_ref, w3_ref, w2_ref, g_ref, b_ref,
                       o_ref, xm_ref, xb_ref, acc_ref, *, n_chunks):
    h = (jnp.dot(oa_ref[...], wa_ref[...], preferred_element_type=F32)
         + jnp.dot(ob_ref[...], wb_ref[...], preferred_element_type=F32))
    xm_ref[...] = _layer_norm_rows(DEEPNORM_ALPHA * x_ref[...] + h, gm_ref[...], bm_ref[...])
    _swiglu_ln(xm_ref, w1_ref, w3_ref, w2_ref, g_ref, b_ref, o_ref, xb_ref, acc_ref, n_chunks)


def _ffn_ln(x, w1, w3, w2, g, b, mix=None):
    t, d = x.shape
    d_ff = w1.shape[1]
    assert d_ff % FFN_CHUNK == 0
    n_chunks = d_ff // FFN_CHUNK
    w1c = w1.astype(BF16).reshape(d, n_chunks, FFN_CHUNK).transpose(1, 0, 2)
    w3c = w3.astype(BF16).reshape(d, n_chunks, FFN_CHUNK).transpose(1, 0, 2)
    w2c = w2.astype(BF16).reshape(n_chunks, FFN_CHUNK, d)
    tm = _token_tile(t, FFN_TOKEN_TILE)
    row_tile = lambda width: pl.BlockSpec((tm, width), lambda i: (i, 0))
    ffn_specs = [
        _const_spec((n_chunks, d, FFN_CHUNK)),
        _const_spec((n_chunks, d, FFN_CHUNK)),
        _const_spec((n_chunks, FFN_CHUNK, d)),
        _const_spec((1, d)),
        _const_spec((1, d)),
    ]
    ffn_args = (w1c, w3c, w2c, g.reshape(1, d), b.reshape(1, d))
    scratch = [pltpu.VMEM((tm, d), BF16), pltpu.VMEM((tm, d), F32)]
    if mix is None:
        body, in_specs, args = _ffn_ln_kernel, [row_tile(d)] + ffn_specs, (x,) + ffn_args
    else:
        oa, ob, wa, wb, gm, bm = mix
        ka, kb = oa.shape[1], ob.shape[1]
        body = _mix_ffn_ln_kernel
        in_specs = [row_tile(d), row_tile(ka), row_tile(kb), _const_spec((ka, d)), _const_spec((kb, d)),
                    _const_spec((1, d)), _const_spec((1, d))] + ffn_specs
        args = (x, oa, ob, wa.astype(BF16), wb.astype(BF16), gm.reshape(1, d), bm.reshape(1, d)) + ffn_args
        scratch = [pltpu.VMEM((tm, d), F32)] + scratch
    return pl.pallas_call(
        functools.partial(body, n_chunks=n_chunks),
        grid=(t // tm,),
        in_specs=in_specs,
        out_specs=row_tile(d),
        out_shape=jax.ShapeDtypeStruct((t, d), F32),
        scratch_shapes=scratch,
        compiler_params=_params("parallel"),
        name="ffn_ln" if mix is None else "mix_ffn_ln",
    )(*args)


def _proj_kernel(x_ref, w_ref, o_ref, *, col_chunk):
    xb = x_ref[...].astype(BF16)
    n = w_ref.shape[1]
    for c0 in range(0, n, col_chunk):
        o_ref[:, c0:c0 + col_chunk] = jnp.dot(
            xb, w_ref[:, c0:c0 + col_chunk], preferred_element_type=F32).astype(o_ref.dtype)


def _proj(x, w, out_dtype):
    t, d = x.shape
    n = w.shape[1]
    col_chunk = 256
    assert n % col_chunk == 0
    tm = _token_tile(t, 512)
    return pl.pallas_call(
        functools.partial(_proj_kernel, col_chunk=col_chunk),
        grid=(t // tm,),
        in_specs=[pl.BlockSpec((tm, d), lambda i: (i, 0)), _const_spec((d, n))],
        out_specs=pl.BlockSpec((tm, n), lambda i: (i, 0)),
        out_shape=jax.ShapeDtypeStruct((t, n), out_dtype),
        compiler_params=_params("parallel"),
        name="in_proj",
    )(x, w.astype(BF16))


def _swa_kernel(sinks_ref, q_ref, kp_ref, kc_ref, vp_ref, vc_ref, o_ref):
    i = pl.program_id(1)
    w = SWA_WINDOW
    kk = jnp.concatenate([kp_ref[0], kc_ref[0]], axis=0)
    vv = jnp.concatenate([vp_ref[0], vc_ref[0]], axis=0)
    qi = lax.broadcasted_iota(jnp.int32, (w, 2 * w), 0)
    kj = lax.broadcasted_iota(jnp.int32, (w, 2 * w), 1)
    dist = qi + w - kj
    valid = (dist >= 0) & (dist < w) & ((i > 0) | (kj >= w))
    distf = dist.astype(F32)
    low_half = lax.broadcasted_iota(jnp.int32, (w, V7X_LANES), 1) < HEAD_DIM
    n_pairs = SWA_Q_HEADS // 2
    tiles = []
    for p in range(n_pairs):
        qp = q_ref[0, :, p * V7X_LANES:(p + 1) * V7X_LANES]
        tiles += [jnp.where(low_half, qp, jnp.zeros_like(qp)), jnp.where(low_half, jnp.zeros_like(qp), qp)]
    s_all = lax.dot_general(jnp.concatenate(tiles, axis=0), kk, (((1,), (1,)), ((), ())),
                            preferred_element_type=F32)
    probs, denoms = [], []
    for j in range(SWA_Q_HEADS):
        p, a = divmod(j, 2)
        h = a * SWA_GROUP + p
        s = s_all[j * w:(j + 1) * w] * ATTN_SCALE
        logits = jnp.where(valid, s - _alibi_slope(h) * distf, -jnp.inf)
        sink = sinks_ref[h]
        m = jnp.maximum(jnp.max(logits, axis=-1, keepdims=True), sink)
        pexp = jnp.exp(logits - m)
        denoms.append(jnp.sum(pexp, axis=-1, keepdims=True) + jnp.exp(sink - m))
        probs.append(pexp.astype(BF16))
    o_all = jnp.dot(jnp.concatenate(probs, axis=0), vv, preferred_element_type=F32)
    for p in range(n_pairs):
        o0 = o_all[(2 * p) * w:(2 * p + 1) * w] / denoms[2 * p]
        o1 = o_all[(2 * p + 1) * w:(2 * p + 2) * w] / denoms[2 * p + 1]
        o_ref[0, :, p * V7X_LANES:(p + 1) * V7X_LANES] = jnp.where(low_half, o0, o1).astype(o_ref.dtype)


def _swa(proj, sinks, col0):
    b, s, _ = proj.shape
    w = SWA_WINDOW
    assert s % w == 0 and col0 % (SWA_Q_HEADS * HEAD_DIM) == 0
    qw = SWA_Q_HEADS * HEAD_DIM
    qblk = col0 // qw
    kblk = (col0 + qw) // V7X_LANES
    vblk = kblk + 1
    return pl.pallas_call(
        _swa_kernel,
        grid=(b, s // w),
        in_specs=[
            pl.BlockSpec(memory_space=pltpu.SMEM),
            pl.BlockSpec((1, w, qw), lambda bi, i: (bi, i, qblk)),
            pl.BlockSpec((1, w, V7X_LANES), lambda bi, i: (bi, jnp.maximum(i - 1, 0), kblk)),
            pl.BlockSpec((1, w, V7X_LANES), lambda bi, i: (bi, i, kblk)),
            pl.BlockSpec((1, w, V7X_LANES), lambda bi, i: (bi, jnp.maximum(i - 1, 0), vblk)),
            pl.BlockSpec((1, w, V7X_LANES), lambda bi, i: (bi, i, vblk)),
        ],
        out_specs=pl.BlockSpec((1, w, qw), lambda bi, i: (bi, i, 0)),
        out_shape=jax.ShapeDtypeStruct((b, s, qw), BF16),
        compiler_params=_params("parallel", "parallel"),
        name="swa",
    )(sinks.astype(F32), proj, proj, proj, proj, proj)


def _split3_bf16(x):
    parts, rest = [], np.float32(x)
    for _ in range(3):
        hi = (np.asarray(rest, np.float32).view(np.uint32) & np.uint32(0xFFFF0000)).view(np.float32)
        parts.append(float(hi))
        rest = np.float32(rest - hi)
    return parts


def _moba_kernel(slopes_ref, q_ref, k_ref, v_ref, o_ref, kmean_ref, rhs_ref, va_ref, lhs_ref, m_ref, acc_ref,
                 sa_ref, sb_ref, *, n_blocks, grp, qb):
    pair = pl.program_id(1)
    ti = pl.program_id(2)
    blk = MOBA_BLOCK
    lanes = V7X_LANES
    half = lanes // 2
    qt = qb * blk
    i0 = ti * qb
    n_sel = min(MOBA_TOPK, n_blocks)

    @pl.when(ti == 0)
    def _():
        kmean_ref[...] = jnp.zeros_like(kmean_ref)
        lane_b = lax.broadcasted_iota(jnp.int32, (blk, lanes), 1)
        low_b = lane_b < HEAD_DIM
        pos_in_blk = lax.broadcasted_iota(jnp.int32, (blk, lanes), 0).astype(F32)
        in_blk_lanes = (lane_b >= half) & (lane_b < half + 3)
        in_pos_lanes = (lane_b >= half + 3) & (lane_b < half + 6)

        def fill(j, carry):
            r0 = pl.multiple_of(j * blk, blk)
            blk_start = lax.convert_element_type(j * blk, F32)
            aug = jnp.where(lane_b == j, 1.0,
                            jnp.where(in_blk_lanes, blk_start, jnp.where(in_pos_lanes, pos_in_blk, 0.0)))
            rhs_ref[pl.ds(r0, blk), :] = jnp.concatenate([k_ref[0, pl.ds(r0, blk), :], aug.astype(BF16)], axis=1)
            vb = v_ref[0, pl.ds(r0, blk), :]
            va_ref[0, pl.ds(r0, blk), :] = jnp.where(low_b, vb, jnp.ones_like(vb))
            va_ref[1, pl.ds(r0, blk), :] = jnp.where(low_b, jnp.ones_like(vb), vb)
            return carry

        lax.fori_loop(0, n_blocks, fill, 0)

    row0 = pl.multiple_of(ti * qt, qt)
    for hb in range(qb):
        k_blk = k_ref[0, pl.ds(row0 + hb * blk, blk), :]
        kmean_ref[pl.ds(i0 + hb, 1), :] = jnp.mean(k_blk.astype(F32), axis=0, keepdims=True)
    kmean_b = kmean_ref[...].astype(BF16)

    lane_id = lax.broadcasted_iota(jnp.int32, (qt, lanes), 1)
    low_half = lane_id < HEAD_DIM
    own_blk = i0 + lax.broadcasted_iota(jnp.int32, (qt, lanes), 0) // blk
    qp = q_ref[0]
    rhs_own = rhs_ref[pl.ds(row0, qt), :]
    r_id = lax.broadcasted_iota(jnp.int32, (qt, qt), 0)
    c_id = lax.broadcasted_iota(jnp.int32, (qt, qt), 1)
    causal = c_id <= r_id
    nt = (((1,), (1,)), ((), ()))
    slope_part = (lane_id - half) % 3

    qms = [jnp.where(low_half, qp, jnp.zeros_like(qp)), jnp.where(low_half, jnp.zeros_like(qp), qp)]
    blk_id = lax.broadcasted_iota(jnp.int32, (half, 2 * qt), 0)
    blk_f = blk_id.astype(F32)
    own_t = i0 + (lax.broadcasted_iota(jnp.int32, (half, 2 * qt), 1) % qt) // blk
    gate = lax.dot_general(kmean_b[:half], jnp.concatenate(qms, axis=0), nt, preferred_element_type=F32)
    gate = jnp.where(blk_id < own_t, gate, -jnp.inf)
    picked_t = jnp.zeros((half, 2 * qt), F32)
    for _ in range(n_sel):
        mx = jnp.max(gate, axis=0, keepdims=True)
        first = jnp.min(jnp.where(gate == mx, blk_f, float(half)), axis=0, keepdims=True)
        pick = (blk_f == first) & (mx > -jnp.inf)
        picked_t = jnp.where(pick, 1.0, picked_t)
        gate = jnp.where(pick, -jnp.inf, gate)
    picked2 = jnp.concatenate([picked_t, jnp.zeros_like(picked_t)], axis=0).T

    for a in range(2):
        s3 = [slopes_ref[6 * pair + 3 * a + t] for t in range(3)]
        qm = qms[a]
        sel_code = jnp.where(picked2[a * qt:(a + 1) * qt] > 0.0, 0.0, MASKED_SCORE)
        slope_lanes = jnp.where(lane_id < half + 6,
                                jnp.where(slope_part == 0, s3[0], jnp.where(slope_part == 1, s3[1], s3[2])), 0.0)
        walk_code = jnp.where(low_half, jnp.where(lane_id >= i0, MASKED_SCORE, sel_code), slope_lanes)
        lhs_ref[a] = jnp.concatenate([qm, walk_code.astype(BF16)], axis=1)
        own_code = jnp.where(low_half, jnp.where(lane_id == own_blk, 0.0, sel_code), slope_lanes)
        lhs_own = jnp.concatenate([qm, own_code.astype(BF16)], axis=1)
        s = lax.dot_general(lhs_own, rhs_own, nt, preferred_element_type=F32)
        logits = jnp.where(causal, s, -jnp.inf)
        m = jnp.max(logits, axis=-1, keepdims=True)
        p = jnp.exp2(logits - m)
        m_ref[a] = jnp.broadcast_to(m, (qt, lanes))
        acc_ref[a] = jnp.dot(p.astype(BF16), va_ref[a, pl.ds(row0, qt), :], preferred_element_type=F32)

    gk = grp * blk
    n_groups = n_blocks // grp

    def scores(g, s_ref):
        r0 = pl.multiple_of(jnp.minimum(g, n_groups - 1) * gk, gk)
        rhs = rhs_ref[pl.ds(r0, gk), :]
        for a in range(2):
            s_ref[a] = lax.dot_general(lhs_ref[a], rhs, nt, preferred_element_type=F32)

    def absorb(g, s_ref):
        r0 = pl.multiple_of(g * gk, gk)
        for a in range(2):
            s = s_ref[a]
            m_old = m_ref[a]
            m_new = jnp.maximum(m_old, jnp.max(s, axis=-1, keepdims=True))
            alpha = jnp.exp2(m_old - m_new)
            p = jnp.exp2(s - jnp.concatenate([m_new] * (gk // lanes), axis=1))
            acc_ref[a] = alpha * acc_ref[a] + jnp.dot(p.astype(BF16), va_ref[a, pl.ds(r0, gk), :],
                                                      preferred_element_type=F32)
            m_ref[a] = m_new

    n_need = (i0 + grp - 1) // grp
    scores(0, sa_ref)

    def group_pair(h, carry):
        g = 2 * h
        scores(g + 1, sb_ref)
        absorb(g, sa_ref)
        scores(g + 2, sa_ref)
        absorb(g + 1, sb_ref)
        return carry

    n_full = jnp.maximum(n_need - 1, 0) // 2
    lax.fori_loop(0, n_full, group_pair, 0)
    g_tail = 2 * n_full

    @pl.when(n_need - g_tail == 1)
    def _():
        absorb(g_tail, sa_ref)

    @pl.when(n_need - g_tail == 2)
    def _():
        scores(g_tail + 1, sb_ref)
        absorb(g_tail, sa_ref)
        absorb(g_tail + 1, sb_ref)

    o0 = acc_ref[0] / pltpu.roll(acc_ref[0], half, 1)
    o1 = acc_ref[1] / pltpu.roll(acc_ref[1], half, 1)
    o_ref[0] = jnp.where(low_half, o0, o1).astype(o_ref.dtype)


def _moba(proj, col0):
    b, s, _ = proj.shape
    blk = MOBA_BLOCK
    assert s % blk == 0 and col0 % V7X_LANES == 0
    n_blocks = s // blk
    assert n_blocks <= V7X_LANES // 2
    grp = MOBA_GROUP
    while n_blocks % grp != 0:
        grp //= 2
    assert grp >= 1
    qb = MOBA_Q_BLOCKS if n_blocks % MOBA_Q_BLOCKS == 0 else 1
    qt = qb * blk
    n_pairs = MOBA_HEADS // 2
    hw = MOBA_HEADS * HEAD_DIM
    qc, kc, vc = col0 // V7X_LANES, (col0 + hw) // V7X_LANES, (col0 + 2 * hw) // V7X_LANES
    slopes = jnp.asarray([part for h in range(MOBA_HEADS)
                          for part in _split3_bf16(_alibi_slope(SWA_Q_HEADS + h) * LOG2_E)], dtype=F32)
    whole_seq = functools.partial(pl.BlockSpec, (1, s, V7X_LANES), pipeline_mode=pl.Buffered(1))
    return pl.pallas_call(
        functools.partial(_moba_kernel, n_blocks=n_blocks, grp=grp, qb=qb),
        grid=(b, n_pairs, n_blocks // qb),
        in_specs=[
            pl.BlockSpec(memory_space=pltpu.SMEM),
            pl.BlockSpec((1, qt, V7X_LANES), lambda bi, p, ti: (bi, ti, qc + p)),
            whole_seq(lambda bi, p, ti: (bi, 0, kc + p)),
            whole_seq(lambda bi, p, ti: (bi, 0, vc + p)),
        ],
        out_specs=pl.BlockSpec((1, qt, V7X_LANES), lambda bi, p, ti: (bi, ti, p)),
        out_shape=jax.ShapeDtypeStruct((b, s, hw), BF16),
        scratch_shapes=[
            pltpu.VMEM((V7X_LANES, V7X_LANES), F32),
            pltpu.VMEM((s, 2 * V7X_LANES), BF16),
            pltpu.VMEM((2, s, V7X_LANES), BF16),
            pltpu.VMEM((2, qt, 2 * V7X_LANES), BF16),
            pltpu.VMEM((2, qt, V7X_LANES), F32),
            pltpu.VMEM((2, qt, V7X_LANES), F32),
            pltpu.VMEM((2, qt, grp * blk), F32),
            pltpu.VMEM((2, qt, grp * blk), F32),
        ],
        compiler_params=_params("arbitrary", "arbitrary", "arbitrary"),
        name="moba",
    )(slopes, proj, proj, proj)


def _block_ref_rows(b, m):
    c, n = b.shape
    if 2 * m >= 8:
        b3 = b.reshape(c // (2 * m), 2 * m, n)
        return jnp.broadcast_to(b3[:, m - 1:m, :], b3.shape).reshape(c, n)
    r = lax.broadcasted_iota(jnp.int32, (c, n), 0)
    if m == 1:
        return jnp.where(r % 2 == 1, pltpu.roll(b, 1, 0), b)
    assert m == 2
    q4 = r % 4
    up1 = pltpu.roll(b, c - 1, 0)
    dn1 = pltpu.roll(b, 1, 0)
    dn2 = pltpu.roll(b, 2, 0)
    return jnp.where(q4 == 0, up1, jnp.where(q4 == 1, b, jnp.where(q4 == 2, dn1, dn2)))


def _neg_abs(x):
    bits = lax.bitcast_convert_type(x, jnp.uint32) | jnp.uint32(0x80000000)
    return lax.bitcast_convert_type(bits, F32)


def _hgrn_layer_kernel(lbp_ref, ng_ref, x_ref, xn_ref, w_in_ref, w_out_ref, g_ref, b_ref, y_ref,
                       st_ref, lv_ref, cur_ref, nxt_ref, o_ref, *, layer, chunk):
    c = chunk
    lanes = V7X_LANES
    ci = pl.program_id(1)
    t_id = lax.broadcasted_iota(jnp.int32, (c, c), 0)
    s_id = lax.broadcasted_iota(jnp.int32, (c, c), 1)

    def project(x_blk, dst_ref):
        dst_ref[...] = jnp.dot(x_blk.astype(BF16), w_in_ref[...], preferred_element_type=F32).astype(BF16)

    @pl.when(ci == 0)
    def _():
        st_ref[...] = jnp.zeros_like(st_ref)
        x = t_id ^ s_id
        code = jnp.zeros((c, c), jnp.int32)
        bit = 1
        while bit < c:
            code = code + (x >= bit).astype(jnp.int32)
            bit *= 2
        lv_ref[...] = jnp.where(s_id <= t_id, code, -1)
        project(x_ref[0], cur_ref)

    project(xn_ref[0], nxt_ref)

    tril = jnp.where(s_id <= t_id, 1.0, 0.0).astype(BF16)
    hw = HGRN_HEADS * lanes
    for hd in range(HGRN_HEADS):
        cols = slice(hd * lanes, (hd + 1) * lanes)
        q, f, v, g = (cur_ref[:, j * hw + hd * lanes:j * hw + (hd + 1) * lanes] for j in range(4))
        o_ref[:, cols] = _hgrn_head(lbp_ref[:, cols], ng_ref[...], q, f, v, g, st_ref.at[hd], lv_ref, tril,
                                    layer=layer, c=c).astype(o_ref.dtype)
    h = jnp.dot(o_ref[...], w_out_ref[...], preferred_element_type=F32)
    y = DEEPNORM_ALPHA * x_ref[0] + h
    y_ref[0] = _layer_norm_rows(y, g_ref[...], b_ref[...])
    cur_ref[...] = nxt_ref[...]


def _hgrn_head(lbp, ng, qraw, fraw, v, graw, st_ref, lv_ref, tril, *, layer, c):
    lanes = V7X_LANES
    lbp = lbp.astype(F32)
    e = jnp.exp(lbp - jnp.max(lbp, axis=0, keepdims=True))
    sm = e / jnp.sum(e, axis=0, keepdims=True)
    lb = jnp.sum(sm[1:layer + 1], axis=0, keepdims=True) if layer >= 1 else jnp.zeros((1, lbp.shape[1]), F32)

    qraw = qraw.astype(F32)
    fraw = fraw.astype(F32)
    v = v.astype(F32)
    graw = graw.astype(F32)
    q = qraw * jax.nn.sigmoid(qraw)
    f_t = lb + (1.0 - lb) * jax.nn.sigmoid(fraw)
    g = jnp.log2(f_t)
    kk = (1.0 - lb) * jax.nn.sigmoid(-fraw)

    g_hi = _bf16_truncate(g)
    r1 = g - g_hi
    g_mid = _bf16_truncate(r1)
    g_lo = r1 - g_mid
    b = (jnp.dot(tril, g_hi.astype(BF16), preferred_element_type=F32)
         + jnp.dot(tril, g_mid.astype(BF16), preferred_element_type=F32)
         + jnp.dot(tril, g_lo.astype(BF16), preferred_element_type=F32))

    nt = (((1,), (1,)), ((), ()))
    st = st_ref[...]
    o = lax.dot_general((q * jnp.exp2(b)).astype(BF16), st.astype(BF16), nt, preferred_element_type=F32)

    prods = {0: lax.dot_general(q.astype(BF16), kk.astype(BF16), nt, preferred_element_type=F32)}
    m, code = 1, 1
    while m < c:
        dec = jnp.exp2(_neg_abs(b - _block_ref_rows(b, m)))
        qe, ke = (q * dec).astype(BF16), (kk * dec).astype(BF16)
        if 2 * m == c and m % lanes == 0:
            prods[code] = lax.dot_general(qe[m:], ke[:m], nt, preferred_element_type=F32)
        else:
            prods[code] = lax.dot_general(qe, ke, nt, preferred_element_type=F32)
        m, code = m * 2, code + 1
    top = code - 1 if (c // 2) % lanes == 0 else None
    bands = []
    for r0 in range(0, c, 8):
        d0 = (r0 // lanes) * lanes
        lv = lv_ref[r0:r0 + 8, d0:d0 + lanes]
        tile = jnp.zeros((8, lanes), F32)
        for cd, p in prods.items():
            if cd == top:
                continue
            mm = 1 << (cd - 1) if cd else 0
            if cd and (r0 // mm) % 2 == 0 and mm >= 8:
                continue
            tile = jnp.where(lv == cd, p[r0:r0 + 8, d0:d0 + lanes], tile)
        cols = []
        for c0 in range(0, c, lanes):
            if c0 == d0:
                cols.append(tile)
            elif top is not None and c0 < d0:
                cols.append(prods[top][r0 - c // 2:r0 - c // 2 + 8, c0:c0 + lanes])
            else:
                cols.append(jnp.zeros((8, lanes), F32))
        bands.append(jnp.concatenate(cols, axis=1))
    a = jnp.concatenate(bands, axis=0)
    o = o + jnp.dot(a.astype(BF16), v.astype(BF16), preferred_element_type=F32)

    b_last = b[c - 1:c, :]
    k_end = (kk * jnp.exp2(b_last - b)).astype(BF16)
    st_ref[...] = st * jnp.exp2(b_last) + jnp.dot(v.T.astype(BF16), k_end, preferred_element_type=F32)

    rms = o * lax.rsqrt(jnp.mean(o * o, axis=-1, keepdims=True) + RMS_EPS) * ng.astype(F32)
    return rms * (graw * jax.nn.sigmoid(graw))


def _swa_head_order():
    order = []
    for p in range(SWA_GROUP):
        for a in range(SWA_KV_HEADS):
            h = a * SWA_GROUP + p
            order.extend(range(h * HEAD_DIM, (h + 1) * HEAD_DIM))
    return jnp.asarray(order, dtype=jnp.int32)


def _attention_mixer(x, batch, w_in, sinks, w_out):
    t, d = x.shape
    s = t // batch
    perm = _swa_head_order()
    qa_w = SWA_Q_HEADS * HEAD_DIM
    kv_w = 2 * SWA_KV_HEADS * HEAD_DIM
    qb_w = MOBA_HEADS * HEAD_DIM
    w_in_p = jnp.concatenate([w_in[:, :qa_w][:, perm], w_in[:, qa_w:qa_w + kv_w],
                              w_in[:, qa_w + kv_w:qa_w + kv_w + qb_w] * (ATTN_SCALE * LOG2_E),
                              w_in[:, qa_w + kv_w + qb_w:]], axis=1)
    proj = _proj(x, w_in_p, BF16).reshape(batch, s, -1)
    oa = _swa(proj, sinks, 0).reshape(t, qa_w)
    ob = _moba(proj, qa_w + kv_w).reshape(t, qb_w)
    return oa, ob, w_out[:qa_w][perm], w_out[qa_w:]


def _hgrn_layer(x, batch, w_in, norm_g, w_out, lower_bounds, layer, g, b):
    t, d = x.shape
    s = t // batch
    c = min(HGRN_CHUNK, s)
    assert s % c == 0 and c % 8 == 0 and (c & (c - 1)) == 0
    assert HGRN_DK == HGRN_DV == V7X_LANES
    nc = s // c
    hw = HGRN_HEADS * HGRN_DK
    depth = lower_bounds.shape[0]
    y = pl.pallas_call(
        functools.partial(_hgrn_layer_kernel, layer=layer, chunk=c),
        grid=(batch, nc),
        in_specs=[
            pl.BlockSpec((depth, hw), lambda bi, ci: (0, 0)),
            pl.BlockSpec((1, HGRN_DV), lambda bi, ci: (0, 0)),
            pl.BlockSpec((1, c, d), lambda bi, ci: (bi, ci, 0)),
            pl.BlockSpec((1, c, d), lambda bi, ci: (bi, jnp.minimum(ci + 1, nc - 1), 0)),
            _const_spec((d, 4 * hw)),
            _const_spec((hw, d)),
            _const_spec((1, d)),
            _const_spec((1, d)),
        ],
        out_specs=pl.BlockSpec((1, c, d), lambda bi, ci: (bi, ci, 0)),
        out_shape=jax.ShapeDtypeStruct((batch, s, d), F32),
        scratch_shapes=[
            pltpu.VMEM((HGRN_HEADS, HGRN_DV, HGRN_DK), F32),
            pltpu.VMEM((c, c), jnp.int32),
            pltpu.VMEM((c, 4 * hw), BF16),
            pltpu.VMEM((c, 4 * hw), BF16),
            pltpu.VMEM((c, hw), BF16),
        ],
        compiler_params=_params("arbitrary", "arbitrary"),
        name="hgrn_layer",
    )(lower_bounds, norm_g.reshape(1, HGRN_DV), x.reshape(batch, s, d), x.reshape(batch, s, d),
      w_in.astype(BF16), w_out.astype(BF16), g.reshape(1, d), b.reshape(1, d))
    return y.reshape(t, d)


def kernel(x, ffn_w1, ffn_w3, ffn_w2, ln_g, ln_b, attn_w_in, attn_sinks, attn_w_out,
           hgrn_w_in, hgrn_norm_g, hgrn_w_out, hgrn_lower_bounds):
    batch, seq, d = x.shape
    xf = x.reshape(batch * seq, d)
    depth = ffn_w1.shape[0]
    for l in range(depth):
        j = l // 2
        xf = _ffn_ln(xf, ffn_w1[l, 0], ffn_w3[l, 0], ffn_w2[l, 0], ln_g[l, 0], ln_b[l, 0])
        if l % 2 == 0:
            mix = _attention_mixer(xf, batch, attn_w_in[j], attn_sinks[j], attn_w_out[j]) + (ln_g[l, 1], ln_b[l, 1])
        else:
            xf = _hgrn_layer(xf, batch, hgrn_w_in[j], hgrn_norm_g[j], hgrn_w_out[j], hgrn_lower_bounds, l,
                             ln_g[l, 1], ln_b[l, 1])
            mix = None
        xf = _ffn_ln(xf, ffn_w1[l, 1], ffn_w3[l, 1], ffn_w2[l, 1], ln_g[l, 2], ln_b[l, 2], mix=mix)
    return xf.reshape(batch, seq, d)
```

```python
import functools

import jax
import jax.numpy as jnp
import numpy as np
from jax import lax
from jax.experimental import pallas as pl
from jax.experimental.pallas import tpu as pltpu

F32 = jnp.float32
BF16 = jnp.bfloat16

DEPTH = 2
HEAD_DIM = 64
SWA_Q_HEADS = 8
SWA_KV_HEADS = 2
SWA_GROUP = SWA_Q_HEADS // SWA_KV_HEADS
SWA_WINDOW = 128
MOBA_HEADS = 8
MOBA_BLOCK = 256
MOBA_TOPK = 3
HGRN_HEADS = 8
HGRN_DK = 128
HGRN_DV = 128
DEEPNORM_ALPHA = (2 * DEPTH) ** 0.25
FFN_RES_WEIGHT = 0.5
LN_EPS = 1e-5
RMS_EPS = 1e-6
ATTN_SCALE = HEAD_DIM ** -0.5
LOG2_E = 1.4426950408889634
N_ATTN_HEADS = SWA_Q_HEADS + MOBA_HEADS

V7X_LANES = 128
V7X_VMEM_BYTES = 64 * 1024 * 1024
VMEM_LIMIT_BYTES = 56 * 1024 * 1024

FFN_CHUNK = 2816
FFN_TOKEN_TILE = 512
HGRN_CHUNK = 256
MOBA_GROUP = 4
MOBA_Q_BLOCKS = 4
MASKED_SCORE = -1e30


def _alibi_slope(i):
    return float(2.0 ** (-8.0 * (i + 1) / N_ATTN_HEADS))


def _token_tile(t, cap):
    tm = min(t, cap)
    assert t % tm == 0, (t, tm)
    return tm


def _params(*sem):
    return pltpu.CompilerParams(dimension_semantics=sem, vmem_limit_bytes=VMEM_LIMIT_BYTES)


def _const_spec(shape):
    return pl.BlockSpec(shape, lambda *_: (0,) * len(shape), pipeline_mode=pl.Buffered(1))


def _bf16_truncate(x):
    bits = lax.bitcast_convert_type(x, jnp.uint32) & jnp.uint32(0xFFFF0000)
    return lax.bitcast_convert_type(bits, F32)


def _layer_norm_rows(y, g, b):
    mu = jnp.mean(y, axis=-1, keepdims=True)
    d = y - mu
    var = jnp.mean(d * d, axis=-1, keepdims=True)
    return d * lax.rsqrt(var + LN_EPS) * g + b


def _swiglu_ln(x_ref, w1_ref, w3_ref, w2_ref, g_ref, b_ref, o_ref, xb_ref, acc_ref, n_chunks):
    xb_ref[...] = x_ref[...].astype(BF16)

    def chunk(c):
        xb = xb_ref[...]
        h1 = jnp.dot(xb, w1_ref[c], preferred_element_type=F32)
        h3 = jnp.dot(xb, w3_ref[c], preferred_element_type=F32)
        h = (h1 * jax.nn.sigmoid(h1)) * h3
        return jnp.dot(h.astype(BF16), w2_ref[c], preferred_element_type=F32)

    acc_ref[...] = chunk(0)

    def more(c, carry):
        acc_ref[...] += chunk(c)
        return carry

    lax.fori_loop(1, n_chunks, more, 0)

    y = DEEPNORM_ALPHA * x_ref[...] + FFN_RES_WEIGHT * acc_ref[...]
    o_ref[...] = _layer_norm_rows(y, g_ref[...], b_ref[...])


def _ffn_ln_kernel(x_ref, w1_ref, w3_ref, w2_ref, g_ref, b_ref, o_ref, xb_ref, acc_ref, *, n_chunks):
    _swiglu_ln(x_ref, w1_ref, w3_ref, w2_ref, g_ref, b_ref, o_ref, xb_ref, acc_ref, n_chunks)


def _mix_ffn_ln_kernel(x_ref, oa_ref, ob_ref, wa_ref, wb_ref, gm_ref, bm_ref, w1_ref, w3_ref, w2_ref, g_ref, b_ref,
                       o_ref, xm_ref, xb_ref, acc_ref, *, n_chunks):
    h = (jnp.dot(oa_ref[...], wa_ref[...], preferred_element_type=F32)
         + jnp.dot(ob_ref[...], wb_ref[...], preferred_element_type=F32))
    xm_ref[...] = _layer_norm_rows(DEEPNORM_ALPHA * x_ref[...] + h, gm_ref[...], bm_ref[...])
    _swiglu_ln(xm_ref, w1_ref, w3_ref, w2_ref, g_ref, b_ref, o_ref, xb_ref, acc_ref, n_chunks)


def _ffn_ln(x, w1, w3, w2, g, b, mix=None):
    t, d = x.shape
    d_ff = w1.shape[1]
    assert d_ff % FFN_CHUNK == 0
    n_chunks = d_ff // FFN_CHUNK
    w1c = w1.astype(BF16).reshape(d, n_chunks, FFN_CHUNK).transpose(1, 0, 2)
    w3c = w3.astype(BF16).reshape(d, n_chunks, FFN_CHUNK).transpose(1, 0, 2)
    w2c = w2.astype(BF16).reshape(n_chunks, FFN_CHUNK, d)
    tm = _token_tile(t, FFN_TOKEN_TILE)
    row_tile = lambda width: pl.BlockSpec((tm, width), lambda i: (i, 0))
    ffn_specs = [
        _const_spec((n_chunks, d, FFN_CHUNK)),
        _const_spec((n_chunks, d, FFN_CHUNK)),
        _const_spec((n_chunks, FFN_CHUNK, d)),
        _const_spec((1, d)),
        _const_spec((1, d)),
    ]
    ffn_args = (w1c, w3c, w2c, g.reshape(1, d), b.reshape(1, d))
    scratch = [pltpu.VMEM((tm, d), BF16), pltpu.VMEM((tm, d), F32)]
    if mix is None:
        body, in_specs, args = _ffn_ln_kernel, [row_tile(d)] + ffn_specs, (x,) + ffn_args
    else:
        oa, ob, wa, wb, gm, bm = mix
        ka, kb = oa.shape[1], ob.shape[1]
        body = _mix_ffn_ln_kernel
        in_specs = [row_tile(d), row_tile(ka), row_tile(kb), _const_spec((ka, d)), _const_spec((kb, d)),
                    _const_spec((1, d)), _const_spec((1, d))] + ffn_specs
        args = (x, oa, ob, wa.astype(BF16), wb.astype(BF16), gm.reshape(1, d), bm.reshape(1, d)) + ffn_args
        scratch = [pltpu.VMEM((tm, d), F32)] + scratch
    return pl.pallas_call(
        functools.partial(body, n_chunks=n_chunks),
        grid=(t // tm,),
        in_specs=in_specs,
        out_specs=row_tile(d),
        out_shape=jax.ShapeDtypeStruct((t, d), F32),
        scratch_shapes=scratch,
        compiler_params=_params("parallel"),
        name="ffn_ln" if mix is None else "mix_ffn_ln",
    )(*args)


def _proj_kernel(x_ref, w_ref, o_ref, *, col_chunk):
    xb = x_ref[...].astype(BF16)
    n = w_ref.shape[1]
    for c0 in range(0, n, col_chunk):
        o_ref[:, c0:c0 + col_chunk] = jnp.dot(
            xb, w_ref[:, c0:c0 + col_chunk], preferred_element_type=F32).astype(o_ref.dtype)


def _proj(x, w, out_dtype):
    t, d = x.shape
    n = w.shape[1]
    col_chunk = 256
    assert n % col_chunk == 0
    tm = _token_tile(t, 512)
    return pl.pallas_call(
        functools.partial(_proj_kernel, col_chunk=col_chunk),
        grid=(t // tm,),
        in_specs=[pl.BlockSpec((tm, d), lambda i: (i, 0)), _const_spec((d, n))],
        out_specs=pl.BlockSpec((tm, n), lambda i: (i, 0)),
        out_shape=jax.ShapeDtypeStruct((t, n), out_dtype),
        compiler_params=_params("parallel"),
        name="in_proj",
    )(x, w.astype(BF16))


def _swa_kernel(sinks_ref, q_ref, kp_ref, kc_ref, vp_ref, vc_ref, o_ref):
    i = pl.program_id(1)
    w = SWA_WINDOW
    kk = jnp.concatenate([kp_ref[0], kc_ref[0]], axis=0)
    vv = jnp.concatenate([vp_ref[0], vc_ref[0]], axis=0)
    qi = lax.broadcasted_iota(jnp.int32, (w, 2 * w), 0)
    kj = lax.broadcasted_iota(jnp.int32, (w, 2 * w), 1)
    dist = qi + w - kj
    valid = (dist >= 0) & (dist < w) & ((i > 0) | (kj >= w))
    distf = dist.astype(F32)
    low_half = lax.broadcasted_iota(jnp.int32, (w, V7X_LANES), 1) < HEAD_DIM
    n_pairs = SWA_Q_HEADS // 2
    tiles = []
    for p in range(n_pairs):
        qp = q_ref[0, :, p * V7X_LANES:(p + 1) * V7X_LANES]
        tiles += [jnp.where(low_half, qp, jnp.zeros_like(qp)), jnp.where(low_half, jnp.zeros_like(qp), qp)]
    s_all = lax.dot_general(jnp.concatenate(tiles, axis=0), kk, (((1,), (1,)), ((), ())),
                            preferred_element_type=F32)
    probs, denoms = [], []
    for j in range(SWA_Q_HEADS):
        p, a = divmod(j, 2)
        h = a * SWA_GROUP + p
        s = s_all[j * w:(j + 1) * w] * ATTN_SCALE
        logits = jnp.where(valid, s - _alibi_slope(h) * distf, -jnp.inf)
        sink = sinks_ref[h]
        m = jnp.maximum(jnp.max(logits, axis=-1, keepdims=True), sink)
        pexp = jnp.exp(logits - m)
        denoms.append(jnp.sum(pexp, axis=-1, keepdims=True) + jnp.exp(sink - m))
        probs.append(pexp.astype(BF16))
    o_all = jnp.dot(jnp.concatenate(probs, axis=0), vv, preferred_element_type=F32)
    for p in range(n_pairs):
        o0 = o_all[(2 * p) * w:(2 * p + 1) * w] / denoms[2 * p]
        o1 = o_all[(2 * p + 1) * w:(2 * p + 2) * w] / denoms[2 * p + 1]
        o_ref[0, :, p * V7X_LANES:(p + 1) * V7X_LANES] = jnp.where(low_half, o0, o1).astype(o_ref.dtype)


def _swa(proj, sinks, col0):
    b, s, _ = proj.shape
    w = SWA_WINDOW
    assert s % w == 0 and col0 % (SWA_Q_HEADS * HEAD_DIM) == 0
    qw = SWA_Q_HEADS * HEAD_DIM
    qblk = col0 // qw
    kblk = (col0 + qw) // V7X_LANES
    vblk = kblk + 1
    return pl.pallas_call(
        _swa_kernel,
        grid=(b, s // w),
        in_specs=[
            pl.BlockSpec(memory_space=pltpu.SMEM),
            pl.BlockSpec((1, w, qw), lambda bi, i: (bi, i, qblk)),
            pl.BlockSpec((1, w, V7X_LANES), lambda bi, i: (bi, jnp.maximum(i - 1, 0), kblk)),
            pl.BlockSpec((1, w, V7X_LANES), lambda bi, i: (bi, i, kblk)),
            pl.BlockSpec((1, w, V7X_LANES), lambda bi, i: (bi, jnp.maximum(i - 1, 0), vblk)),
            pl.BlockSpec((1, w, V7X_LANES), lambda bi, i: (bi, i, vblk)),
        ],
        out_specs=pl.BlockSpec((1, w, qw), lambda bi, i: (bi, i, 0)),
        out_shape=jax.ShapeDtypeStruct((b, s, qw), BF16),
        compiler_params=_params("parallel", "parallel"),
        name="swa",
    )(sinks.astype(F32), proj, proj, proj, proj, proj)


def _split3_bf16(x):
    parts, rest = [], np.float32(x)
    for _ in range(3):
        hi = (np.asarray(rest, np.float32).view(np.uint32) & np.uint32(0xFFFF0000)).view(np.float32)
        parts.append(float(hi))
        rest = np.float32(rest - hi)
    return parts


def _moba_kernel(slopes_ref, q_ref, k_ref, v_ref, o_ref, kmean_ref, rhs_ref, va_ref, lhs_ref, m_ref, acc_ref,
                 sa_ref, sb_ref, *, n_blocks, grp, qb):
    pair = pl.program_id(1)
    ti = pl.program_id(2)
    blk = MOBA_BLOCK
    lanes = V7X_LANES
    half = lanes // 2
    qt = qb * blk
    i0 = ti * qb
    n_sel = min(MOBA_TOPK, n_blocks)

    @pl.when(ti == 0)
    def _():
        kmean_ref[...] = jnp.zeros_like(kmean_ref)
        lane_b = lax.broadcasted_iota(jnp.int32, (blk, lanes), 1)
        low_b = lane_b < HEAD_DIM
        pos_in_blk = lax.broadcasted_iota(jnp.int32, (blk, lanes), 0).astype(F32)
        in_blk_lanes = (lane_b >= half) & (lane_b < half + 3)
        in_pos_lanes = (lane_b >= half + 3) & (lane_b < half + 6)

        def fill(j, carry):
            r0 = pl.multiple_of(j * blk, blk)
            blk_start = lax.convert_element_type(j * blk, F32)
            aug = jnp.where(lane_b == j, 1.0,
                            jnp.where(in_blk_lanes, blk_start, jnp.where(in_pos_lanes, pos_in_blk, 0.0)))
            rhs_ref[pl.ds(r0, blk), :] = jnp.concatenate([k_ref[0, pl.ds(r0, blk), :], aug.astype(BF16)], axis=1)
            vb = v_ref[0, pl.ds(r0, blk), :]
            va_ref[0, pl.ds(r0, blk), :] = jnp.where(low_b, vb, jnp.ones_like(vb))
            va_ref[1, pl.ds(r0, blk), :] = jnp.where(low_b, jnp.ones_like(vb), vb)
            return carry

        lax.fori_loop(0, n_blocks, fill, 0)

    row0 = pl.multiple_of(ti * qt, qt)
    for hb in range(qb):
        k_blk = k_ref[0, pl.ds(row0 + hb * blk, blk), :]
        kmean_ref[pl.ds(i0 + hb, 1), :] = jnp.mean(k_blk.astype(F32), axis=0, keepdims=True)
    kmean_b = kmean_ref[...].astype(BF16)

    lane_id = lax.broadcasted_iota(jnp.int32, (qt, lanes), 1)
    low_half = lane_id < HEAD_DIM
    own_blk = i0 + lax.broadcasted_iota(jnp.int32, (qt, lanes), 0) // blk
    qp = q_ref[0]
    rhs_own = rhs_ref[pl.ds(row0, qt), :]
    r_id = lax.broadcasted_iota(jnp.int32, (qt, qt), 0)
    c_id = lax.broadcasted_iota(jnp.int32, (qt, qt), 1)
    causal = c_id <= r_id
    nt = (((1,), (1,)), ((), ()))
    slope_part = (lane_id - half) % 3

    qms = [jnp.where(low_half, qp, jnp.zeros_like(qp)), jnp.where(low_half, jnp.zeros_like(qp), qp)]
    blk_id = lax.broadcasted_iota(jnp.int32, (half, 2 * qt), 0)
    blk_f = blk_id.astype(F32)
    own_t = i0 + (lax.broadcasted_iota(jnp.int32, (half, 2 * qt), 1) % qt) // blk
    gate = lax.dot_general(kmean_b[:half], jnp.concatenate(qms, axis=0), nt, preferred_element_type=F32)
    gate = jnp.where(blk_id < own_t, gate, -jnp.inf)
    picked_t = jnp.zeros((half, 2 * qt), F32)
    for _ in range(n_sel):
        mx = jnp.max(gate, axis=0, keepdims=True)
        first = jnp.min(jnp.where(gate == mx, blk_f, float(half)), axis=0, keepdims=True)
        pick = (blk_f == first) & (mx > -jnp.inf)
        picked_t = jnp.where(pick, 1.0, picked_t)
        gate = jnp.where(pick, -jnp.inf, gate)
    picked2 = jnp.concatenate([picked_t, jnp.zeros_like(picked_t)], axis=0).T

    for a in range(2):
        s3 = [slopes_ref[6 * pair + 3 * a + t] for t in range(3)]
        qm = qms[a]
        sel_code = jnp.where(picked2[a * qt:(a + 1) * qt] > 0.0, 0.0, MASKED_SCORE)
        slope_lanes = jnp.where(lane_id < half + 6,
                                jnp.where(slope_part == 0, s3[0], jnp.where(slope_part == 1, s3[1], s3[2])), 0.0)
        walk_code = jnp.where(low_half, jnp.where(lane_id >= i0, MASKED_SCORE, sel_code), slope_lanes)
        lhs_ref[a] = jnp.concatenate([qm, walk_code.astype(BF16)], axis=1)
        own_code = jnp.where(low_half, jnp.where(lane_id == own_blk, 0.0, sel_code), slope_lanes)
        lhs_own = jnp.concatenate([qm, own_code.astype(BF16)], axis=1)
        s = lax.dot_general(lhs_own, rhs_own, nt, preferred_element_type=F32)
        logits = jnp.where(causal, s, -jnp.inf)
        m = jnp.max(logits, axis=-1, keepdims=True)
        p = jnp.exp2(logits - m)
        m_ref[a] = jnp.broadcast_to(m, (qt, lanes))
        acc_ref[a] = jnp.dot(p.astype(BF16), va_ref[a, pl.ds(row0, qt), :], preferred_element_type=F32)

    gk = grp * blk
    n_groups = n_blocks // grp

    def scores(g, s_ref):
        r0 = pl.multiple_of(jnp.minimum(g, n_groups - 1) * gk, gk)
        rhs = rhs_ref[pl.ds(r0, gk), :]
        for a in range(2):
            s_ref[a] = lax.dot_general(lhs_ref[a], rhs, nt, preferred_element_type=F32)

    def absorb(g, s_ref):
        r0 = pl.multiple_of(g * gk, gk)
        for a in range(2):
            s = s_ref[a]
            m_old = m_ref[a]
            m_new = jnp.maximum(m_old, jnp.max(s, axis=-1, keepdims=True))
            alpha = jnp.exp2(m_old - m_new)
            p = jnp.exp2(s - jnp.concatenate([m_new] * (gk // lanes), axis=1))
            acc_ref[a] = alpha * acc_ref[a] + jnp.dot(p.astype(BF16), va_ref[a, pl.ds(r0, gk), :],
                                                      preferred_element_type=F32)
            m_ref[a] = m_new

    n_need = (i0 + grp - 1) // grp
    scores(0, sa_ref)

    def group_pair(h, carry):
        g = 2 * h
        scores(g + 1, sb_ref)
        absorb(g, sa_ref)
        scores(g + 2, sa_ref)
        absorb(g + 1, sb_ref)
        return carry

    n_full = jnp.maximum(n_need - 1, 0) // 2
    lax.fori_loop(0, n_full, group_pair, 0)
    g_tail = 2 * n_full

    @pl.when(n_need - g_tail == 1)
    def _():
        absorb(g_tail, sa_ref)

    @pl.when(n_need - g_tail == 2)
    def _():
        scores(g_tail + 1, sb_ref)
        absorb(g_tail, sa_ref)
        absorb(g_tail + 1, sb_ref)

    o0 = acc_ref[0] / pltpu.roll(acc_ref[0], half, 1)
    o1 = acc_ref[1] / pltpu.roll(acc_ref[1], half, 1)
    o_ref[0] = jnp.where(low_half, o0, o1).astype(o_ref.dtype)


def _moba(proj, col0):
    b, s, _ = proj.shape
    blk = MOBA_BLOCK
    assert s % blk == 0 and col0 % V7X_LANES == 0
    n_blocks = s // blk
    assert n_blocks <= V7X_LANES // 2
    grp = MOBA_GROUP
    while n_blocks % grp != 0:
        grp //= 2
    assert grp >= 1
    qb = MOBA_Q_BLOCKS if n_blocks % MOBA_Q_BLOCKS == 0 else 1
    qt = qb * blk
    n_pairs = MOBA_HEADS // 2
    hw = MOBA_HEADS * HEAD_DIM
    qc, kc, vc = col0 // V7X_LANES, (col0 + hw) // V7X_LANES, (col0 + 2 * hw) // V7X_LANES
    slopes = jnp.asarray([part for h in range(MOBA_HEADS)
                          for part in _split3_bf16(_alibi_slope(SWA_Q_HEADS + h) * LOG2_E)], dtype=F32)
    whole_seq = functools.partial(pl.BlockSpec, (1, s, V7X_LANES), pipeline_mode=pl.Buffered(1))
    return pl.pallas_call(
        functools.partial(_moba_kernel, n_blocks=n_blocks, grp=grp, qb=qb),
        grid=(b, n_pairs, n_blocks // qb),
        in_specs=[
            pl.BlockSpec(memory_space=pltpu.SMEM),
            pl.BlockSpec((1, qt, V7X_LANES), lambda bi, p, ti: (bi, ti, qc + p)),
            whole_seq(lambda bi, p, ti: (bi, 0, kc + p)),
            whole_seq(lambda bi, p, ti: (bi, 0, vc + p)),
        ],
        out_specs=pl.BlockSpec((1, qt, V7X_LANES), lambda bi, p, ti: (bi, ti, p)),
        out_shape=jax.ShapeDtypeStruct((b, s, hw), BF16),
        scratch_shapes=[
            pltpu.VMEM((V7X_LANES, V7X_LANES), F32),
            pltpu.VMEM((s, 2 * V7X_LANES), BF16),
            pltpu.VMEM((2, s, V7X_LANES), BF16),
            pltpu.VMEM((2, qt, 2 * V7X_LANES), BF16),
            pltpu.VMEM((2, qt, V7X_LANES), F32),
            pltpu.VMEM((2, qt, V7X_LANES), F32),
            pltpu.VMEM((2, qt, grp * blk), F32),
            pltpu.VMEM((2, qt, grp * blk), F32),
        ],
        compiler_params=_params("arbitrary", "arbitrary", "arbitrary"),
        name="moba",
    )(slopes, proj, proj, proj)


def _block_ref_rows(b, m):
    c, n = b.shape
    if 2 * m >= 8:
        b3 = b.reshape(c // (2 * m), 2 * m, n)
        return jnp.broadcast_to(b3[:, m - 1:m, :], b3.shape).reshape(c, n)
    r = lax.broadcasted_iota(jnp.int32, (c, n), 0)
    if m == 1:
        return jnp.where(r % 2 == 1, pltpu.roll(b, 1, 0), b)
    assert m == 2
    q4 = r % 4
    up1 = pltpu.roll(b, c - 1, 0)
    dn1 = pltpu.roll(b, 1, 0)
    dn2 = pltpu.roll(b, 2, 0)
    return jnp.where(q4 == 0, up1, jnp.where(q4 == 1, b, jnp.where(q4 == 2, dn1, dn2)))


def _neg_abs(x):
    bits = lax.bitcast_convert_type(x, jnp.uint32) | jnp.uint32(0x80000000)
    return lax.bitcast_convert_type(bits, F32)


def _hgrn_layer_kernel(lbp_ref, ng_ref, x_ref, xn_ref, w_in_ref, w_out_ref, g_ref, b_ref, y_ref,
                       st_ref, lv_ref, cur_ref, nxt_ref, o_ref, *, layer, chunk):
    c = chunk
    lanes = V7X_LANES
    ci = pl.program_id(1)
    t_id = lax.broadcasted_iota(jnp.int32, (c, c), 0)
    s_id = lax.broadcasted_iota(jnp.int32, (c, c), 1)

    def project(x_blk, dst_ref):
        dst_ref[...] = jnp.dot(x_blk.astype(BF16), w_in_ref[...], preferred_element_type=F32).astype(BF16)

    @pl.when(ci == 0)
    def _():
        st_ref[...] = jnp.zeros_like(st_ref)
        x = t_id ^ s_id
        code = jnp.zeros((c, c), jnp.int32)
        bit = 1
        while bit < c:
            code = code + (x >= bit).astype(jnp.int32)
            bit *= 2
        lv_ref[...] = jnp.where(s_id <= t_id, code, -1)
        project(x_ref[0], cur_ref)

    project(xn_ref[0], nxt_ref)

    tril = jnp.where(s_id <= t_id, 1.0, 0.0).astype(BF16)
    hw = HGRN_HEADS * lanes
    for hd in range(HGRN_HEADS):
        cols = slice(hd * lanes, (hd + 1) * lanes)
        q, f, v, g = (cur_ref[:, j * hw + hd * lanes:j * hw + (hd + 1) * lanes] for j in range(4))
        o_ref[:, cols] = _hgrn_head(lbp_ref[:, cols], ng_ref[...], q, f, v, g, st_ref.at[hd], lv_ref, tril,
                                    layer=layer, c=c).astype(o_ref.dtype)
    h = jnp.dot(o_ref[...], w_out_ref[...], preferred_element_type=F32)
    y = DEEPNORM_ALPHA * x_ref[0] + h
    y_ref[0] = _layer_norm_rows(y, g_ref[...], b_ref[...])
    cur_ref[...] = nxt_ref[...]


def _hgrn_head(lbp, ng, qraw, fraw, v, graw, st_ref, lv_ref, tril, *, layer, c):
    lanes = V7X_LANES
    lbp = lbp.astype(F32)
    e = jnp.exp(lbp - jnp.max(lbp, axis=0, keepdims=True))
    sm = e / jnp.sum(e, axis=0, keepdims=True)
    lb = jnp.sum(sm[1:layer + 1], axis=0, keepdims=True) if layer >= 1 else jnp.zeros((1, lbp.shape[1]), F32)

    qraw = qraw.astype(F32)
    fraw = fraw.astype(F32)
    v = v.astype(F32)
    graw = graw.astype(F32)
    q = qraw * jax.nn.sigmoid(qraw)
    f_t = lb + (1.0 - lb) * jax.nn.sigmoid(fraw)
    g = jnp.log2(f_t)
    kk = (1.0 - lb) * jax.nn.sigmoid(-fraw)

    g_hi = _bf16_truncate(g)
    r1 = g - g_hi
    g_mid = _bf16_truncate(r1)
    g_lo = r1 - g_mid
    b = (jnp.dot(tril, g_hi.astype(BF16), preferred_element_type=F32)
         + jnp.dot(tril, g_mid.astype(BF16), preferred_element_type=F32)
         + jnp.dot(tril, g_lo.astype(BF16), preferred_element_type=F32))

    nt = (((1,), (1,)), ((), ()))
    st = st_ref[...]
    o = lax.dot_general((q * jnp.exp2(b)).astype(BF16), st.astype(BF16), nt, preferred_element_type=F32)

    prods = {0: lax.dot_general(q.astype(BF16), kk.astype(BF16), nt, preferred_element_type=F32)}
    m, code = 1, 1
    while m < c:
        dec = jnp.exp2(_neg_abs(b - _block_ref_rows(b, m)))
        qe, ke = (q * dec).astype(BF16), (kk * dec).astype(BF16)
        if 2 * m == c and m % lanes == 0:
            prods[code] = lax.dot_general(qe[m:], ke[:m], nt, preferred_element_type=F32)
        else:
            prods[code] = lax.dot_general(qe, ke, nt, preferred_element_type=F32)
        m, code = m * 2, code + 1
    top = code - 1 if (c // 2) % lanes == 0 else None
    bands = []
    for r0 in range(0, c, 8):
        d0 = (r0 // lanes) * lanes
        lv = lv_ref[r0:r0 + 8, d0:d0 + lanes]
        tile = jnp.zeros((8, lanes), F32)
        for cd, p in prods.items():
            if cd == top:
                continue
            mm = 1 << (cd - 1) if cd else 0
            if cd and (r0 // mm) % 2 == 0 and mm >= 8:
                continue
            tile = jnp.where(lv == cd, p[r0:r0 + 8, d0:d0 + lanes], tile)
        cols = []
        for c0 in range(0, c, lanes):
            if c0 == d0:
                cols.append(tile)
            elif top is not None and c0 < d0:
                cols.append(prods[top][r0 - c // 2:r0 - c // 2 + 8, c0:c0 + lanes])
            else:
                cols.append(jnp.zeros((8, lanes), F32))
        bands.append(jnp.concatenate(cols, axis=1))
    a = jnp.concatenate(bands, axis=0)
    o = o + jnp.dot(a.astype(BF16), v.astype(BF16), preferred_element_type=F32)

    b_last = b[c - 1:c, :]
    k_end = (kk * jnp.exp2(b_last - b)).astype(BF16)
    st_ref[...] = st * jnp.exp2(b_last) + jnp.dot(v.T.astype(BF16), k_end, preferred_element_type=F32)

    rms = o * lax.rsqrt(jnp.mean(o * o, axis=-1, keepdims=True) + RMS_EPS) * ng.astype(F32)
    return rms * (graw * jax.nn.sigmoid(graw))


def _swa_head_order():
    order = []
    for p in range(SWA_GROUP):
        for a in range(SWA_KV_HEADS):
            h = a * SWA_GROUP + p
            order.extend(range(h * HEAD_DIM, (h + 1) * HEAD_DIM))
    return jnp.asarray(order, dtype=jnp.int32)


def _attention_mixer(x, batch, w_in, sinks, w_out):
    t, d = x.shape
    s = t // batch
    perm = _swa_head_order()
    qa_w = SWA_Q_HEADS * HEAD_DIM
    kv_w = 2 * SWA_KV_HEADS * HEAD_DIM
    qb_w = MOBA_HEADS * HEAD_DIM
    w_in_p = jnp.concatenate([w_in[:, :qa_w][:, perm], w_in[:, qa_w:qa_w + kv_w],
                              w_in[:, qa_w + kv_w:qa_w + kv_w + qb_w] * (ATTN_SCALE * LOG2_E),
                              w_in[:, qa_w + kv_w + qb_w:]], axis=1)
    proj = _proj(x, w_in_p, BF16).reshape(batch, s, -1)
    oa = _swa(proj, sinks, 0).reshape(t, qa_w)
    ob = _moba(proj, qa_w + kv_w).reshape(t, qb_w)
    return oa, ob, w_out[:qa_w][perm], w_out[qa_w:]


def _hgrn_layer(x, batch, w_in, norm_g, w_out, lower_bounds, layer, g, b):
    t, d = x.shape
    s = t // batch
    c = min(HGRN_CHUNK, s)
    assert s % c == 0 and c % 8 == 0 and (c & (c - 1)) == 0
    assert HGRN_DK == HGRN_DV == V7X_LANES
    nc = s // c
    hw = HGRN_HEADS * HGRN_DK
    depth = lower_bounds.shape[0]
    y = pl.pallas_call(
        functools.partial(_hgrn_layer_kernel, layer=layer, chunk=c),
        grid=(batch, nc),
        in_specs=[
            pl.BlockSpec((depth, hw), lambda bi, ci: (0, 0)),
            pl.BlockSpec((1, HGRN_DV), lambda bi, ci: (0, 0)),
            pl.BlockSpec((1, c, d), lambda bi, ci: (bi, ci, 0)),
            pl.BlockSpec((1, c, d), lambda bi, ci: (bi, jnp.minimum(ci + 1, nc - 1), 0)),
            _const_spec((d, 4 * hw)),
            _const_spec((hw, d)),
            _const_spec((1, d)),
            _const_spec((1, d)),
        ],
        out_specs=pl.BlockSpec((1, c, d), lambda bi, ci: (bi, ci, 0)),
        out_shape=jax.ShapeDtypeStruct((batch, s, d), F32),
        scratch_shapes=[
            pltpu.VMEM((HGRN_HEADS, HGRN_DV, HGRN_DK), F32),
            pltpu.VMEM((c, c), jnp.int32),
            pltpu.VMEM((c, 4 * hw), BF16),
            pltpu.VMEM((c, 4 * hw), BF16),
            pltpu.VMEM((c, hw), BF16),
        ],
        compiler_params=_params("arbitrary", "arbitrary"),
        name="hgrn_layer",
    )(lower_bounds, norm_g.reshape(1, HGRN_DV), x.reshape(batch, s, d), x.reshape(batch, s, d),
      w_in.astype(BF16), w_out.astype(BF16), g.reshape(1, d), b.reshape(1, d))
    return y.reshape(t, d)


def kernel(x, ffn_w1, ffn_w3, ffn_w2, ln_g, ln_b, attn_w_in, attn_sinks, attn_w_out,
           hgrn_w_in, hgrn_norm_g, hgrn_w_out, hgrn_lower_bounds):
    batch, seq, d = x.shape
    xf = x.reshape(batch * seq, d)
    depth = ffn_w1.shape[0]
    for l in range(depth):
        j = l // 2
        xf = _ffn_ln(xf, ffn_w1[l, 0], ffn_w3[l, 0], ffn_w2[l, 0], ln_g[l, 0], ln_b[l, 0])
        if l % 2 == 0:
            mix = _attention_mixer(xf, batch, attn_w_in[j], attn_sinks[j], attn_w_out[j]) + (ln_g[l, 1], ln_b[l, 1])
        else:
            xf = _hgrn_layer(xf, batch, hgrn_w_in[j], hgrn_norm_g[j], hgrn_w_out[j], hgrn_lower_bounds, l,
                             ln_g[l, 1], ln_b[l, 1])
            mix = None
        xf = _ffn_ln(xf, ffn_w1[l, 1], ffn_w3[l, 1], ffn_w2[l, 1], ln_g[l, 2], ln_b[l, 2], mix=mix)
    return xf.reshape(batch, seq, d)
```

```python
import functools

import jax
import jax.numpy as jnp
import numpy as np
from jax import lax
from jax.experimental import pallas as pl
from jax.experimental.pallas import tpu as pltpu

F32 = jnp.float32
BF16 = jnp.bfloat16

DEPTH = 2
HEAD_DIM = 64
SWA_Q_HEADS = 8
SWA_KV_HEADS = 2
SWA_GROUP = SWA_Q_HEADS // SWA_KV_HEADS
SWA_WINDOW = 128
MOBA_HEADS = 8
MOBA_BLOCK = 256
MOBA_TOPK = 3
HGRN_HEADS = 8
HGRN_DK = 128
HGRN_DV = 128
DEEPNORM_ALPHA = (2 * DEPTH) ** 0.25
FFN_RES_WEIGHT = 0.5
LN_EPS = 1e-5
RMS_EPS = 1e-6
ATTN_SCALE = HEAD_DIM ** -0.5
LOG2_E = 1.4426950408889634
N_ATTN_HEADS = SWA_Q_HEADS + MOBA_HEADS

V7X_LANES = 128
V7X_VMEM_BYTES = 64 * 1024 * 1024
VMEM_LIMIT_BYTES = 56 * 1024 * 1024

FFN_CHUNK = 2816
FFN_TOKEN_TILE = 1024
HGRN_CHUNK = 256
MOBA_GROUP = 4
MOBA_Q_BLOCKS = 4
MASKED_SCORE = -1e30


def _alibi_slope(i):
    return float(2.0 ** (-8.0 * (i + 1) / N_ATTN_HEADS))


def _token_tile(t, cap):
    tm = min(t, cap)
    assert t % tm == 0, (t, tm)
    return tm


def _params(*sem):
    return pltpu.CompilerParams(dimension_semantics=sem, vmem_limit_bytes=VMEM_LIMIT_BYTES)


def _const_spec(shape):
    return pl.BlockSpec(shape, lambda *_: (0,) * len(shape), pipeline_mode=pl.Buffered(1))


def _bf16_truncate(x):
    bits = lax.bitcast_convert_type(x, jnp.uint32) & jnp.uint32(0xFFFF0000)
    return lax.bitcast_convert_type(bits, F32)


def _layer_norm_rows(y, g, b):
    mu = jnp.mean(y, axis=-1, keepdims=True)
    d = y - mu
    var = jnp.mean(d * d, axis=-1, keepdims=True)
    return d * lax.rsqrt(var + LN_EPS) * g + b


def _swiglu_ln(x_ref, w1_ref, w3_ref, w2_ref, g_ref, b_ref, o_ref, xb_ref, acc_ref, n_chunks):
    xb_ref[...] = x_ref[...].astype(BF16)

    def chunk(c):
        xb = xb_ref[...]
        h1 = jnp.dot(xb, w1_ref[c], preferred_element_type=F32)
        h3 = jnp.dot(xb, w3_ref[c], preferred_element_type=F32)
        h = (h1 * jax.nn.sigmoid(h1)) * h3
        return jnp.dot(h.astype(BF16), w2_ref[c], preferred_element_type=F32)

    acc_ref[...] = chunk(0)

    def more(c, carry):
        acc_ref[...] += chunk(c)
        return carry

    lax.fori_loop(1, n_chunks, more, 0)

    y = DEEPNORM_ALPHA * x_ref[...] + FFN_RES_WEIGHT * acc_ref[...]
    o_ref[...] = _layer_norm_rows(y, g_ref[...], b_ref[...])


def _ffn_ln_kernel(x_ref, w1_ref, w3_ref, w2_ref, g_ref, b_ref, o_ref, xb_ref, acc_ref, *, n_chunks):
    _swiglu_ln(x_ref, w1_ref, w3_ref, w2_ref, g_ref, b_ref, o_ref, xb_ref, acc_ref, n_chunks)


def _mix_ffn_ln_kernel(x_ref, oa_ref, ob_ref, wa_ref, wb_ref, gm_ref, bm_ref, w1_ref, w3_ref, w2_ref, g_ref, b_ref,
                       o_ref, xm_ref, xb_ref, acc_ref, *, n_chunks):
    h = (jnp.dot(oa_ref[...], wa_ref[...], preferred_element_type=F32)
         + jnp.dot(ob_ref[...], wb_ref[...], preferred_element_type=F32))
    xm_ref[...] = _layer_norm_rows(DEEPNORM_ALPHA * x_ref[...] + h, gm_ref[...], bm_ref[...])
    _swiglu_ln(xm_ref, w1_ref, w3_ref, w2_ref, g_ref, b_ref, o_ref, xb_ref, acc_ref, n_chunks)


def _ffn_ln(x, w1, w3, w2, g, b, mix=None):
    t, d = x.shape
    d_ff = w1.shape[1]
    assert d_ff % FFN_CHUNK == 0
    n_chunks = d_ff // FFN_CHUNK
    w1c = w1.astype(BF16).reshape(d, n_chunks, FFN_CHUNK).transpose(1, 0, 2)
    w3c = w3.astype(BF16).reshape(d, n_chunks, FFN_CHUNK).transpose(1, 0, 2)
    w2c = w2.astype(BF16).reshape(n_chunks, FFN_CHUNK, d)
    tm = _token_tile(t, FFN_TOKEN_TILE)
    row_tile = lambda width: pl.BlockSpec((tm, width), lambda i: (i, 0))
    ffn_specs = [
        _const_spec((n_chunks, d, FFN_CHUNK)),
        _const_spec((n_chunks, d, FFN_CHUNK)),
        _const_spec((n_chunks, FFN_CHUNK, d)),
        _const_spec((1, d)),
        _const_spec((1, d)),
    ]
    ffn_args = (w1c, w3c, w2c, g.reshape(1, d), b.reshape(1, d))
    scratch = [pltpu.VMEM((tm, d), BF16), pltpu.VMEM((tm, d), F32)]
    if mix is None:
        body, in_specs, args = _ffn_ln_kernel, [row_tile(d)] + ffn_specs, (x,) + ffn_args
    else:
        oa, ob, wa, wb, gm, bm = mix
        ka, kb = oa.shape[1], ob.shape[1]
        body = _mix_ffn_ln_kernel
        in_specs = [row_tile(d), row_tile(ka), row_tile(kb), _const_spec((ka, d)), _const_spec((kb, d)),
                    _const_spec((1, d)), _const_spec((1, d))] + ffn_specs
        args = (x, oa, ob, wa.astype(BF16), wb.astype(BF16), gm.reshape(1, d), bm.reshape(1, d)) + ffn_args
        scratch = [pltpu.VMEM((tm, d), F32)] + scratch
    return pl.pallas_call(
        functools.partial(body, n_chunks=n_chunks),
        grid=(t // tm,),
        in_specs=in_specs,
        out_specs=row_tile(d),
        out_shape=jax.ShapeDtypeStruct((t, d), F32),
        scratch_shapes=scratch,
        compiler_params=_params("parallel"),
        name="ffn_ln" if mix is None else "mix_ffn_ln",
    )(*args)


def _proj_kernel(x_ref, w_ref, o_ref, *, col_chunk):
    xb = x_ref[...].astype(BF16)
    n = w_ref.shape[1]
    for c0 in range(0, n, col_chunk):
        o_ref[:, c0:c0 + col_chunk] = jnp.dot(
            xb, w_ref[:, c0:c0 + col_chunk], preferred_element_type=F32).astype(o_ref.dtype)


def _proj(x, w, out_dtype):
    t, d = x.shape
    n = w.shape[1]
    col_chunk = 256
    assert n % col_chunk == 0
    tm = _token_tile(t, 512)
    return pl.pallas_call(
        functools.partial(_proj_kernel, col_chunk=col_chunk),
        grid=(t // tm,),
        in_specs=[pl.BlockSpec((tm, d), lambda i: (i, 0)), _const_spec((d, n))],
        out_specs=pl.BlockSpec((tm, n), lambda i: (i, 0)),
        out_shape=jax.ShapeDtypeStruct((t, n), out_dtype),
        compiler_params=_params("parallel"),
        name="in_proj",
    )(x, w.astype(BF16))


def _swa_kernel(sinks_ref, q_ref, kp_ref, kc_ref, vp_ref, vc_ref, o_ref):
    i = pl.program_id(1)
    w = SWA_WINDOW
    kk = jnp.concatenate([kp_ref[0], kc_ref[0]], axis=0)
    vv = jnp.concatenate([vp_ref[0], vc_ref[0]], axis=0)
    qi = lax.broadcasted_iota(jnp.int32, (w, 2 * w), 0)
    kj = lax.broadcasted_iota(jnp.int32, (w, 2 * w), 1)
    dist = qi + w - kj
    valid = (dist >= 0) & (dist < w) & ((i > 0) | (kj >= w))
    distf = dist.astype(F32)
    low_half = lax.broadcasted_iota(jnp.int32, (w, V7X_LANES), 1) < HEAD_DIM
    n_pairs = SWA_Q_HEADS // 2
    tiles = []
    for p in range(n_pairs):
        qp = q_ref[0, :, p * V7X_LANES:(p + 1) * V7X_LANES]
        tiles += [jnp.where(low_half, qp, jnp.zeros_like(qp)), jnp.where(low_half, jnp.zeros_like(qp), qp)]
    s_all = lax.dot_general(jnp.concatenate(tiles, axis=0), kk, (((1,), (1,)), ((), ())),
                            preferred_element_type=F32)
    probs, denoms = [], []
    for j in range(SWA_Q_HEADS):
        p, a = divmod(j, 2)
        h = a * SWA_GROUP + p
        s = s_all[j * w:(j + 1) * w] * ATTN_SCALE
        logits = jnp.where(valid, s - _alibi_slope(h) * distf, -jnp.inf)
        sink = sinks_ref[h]
        m = jnp.maximum(jnp.max(logits, axis=-1, keepdims=True), sink)
        pexp = jnp.exp(logits - m)
        denoms.append(jnp.sum(pexp, axis=-1, keepdims=True) + jnp.exp(sink - m))
        probs.append(pexp.astype(BF16))
    o_all = jnp.dot(jnp.concatenate(probs, axis=0), vv, preferred_element_type=F32)
    for p in range(n_pairs):
        o0 = o_all[(2 * p) * w:(2 * p + 1) * w] / denoms[2 * p]
        o1 = o_all[(2 * p + 1) * w:(2 * p + 2) * w] / denoms[2 * p + 1]
        o_ref[0, :, p * V7X_LANES:(p + 1) * V7X_LANES] = jnp.where(low_half, o0, o1).astype(o_ref.dtype)


def _swa(proj, sinks, col0):
    b, s, _ = proj.shape
    w = SWA_WINDOW
    assert s % w == 0 and col0 % (SWA_Q_HEADS * HEAD_DIM) == 0
    qw = SWA_Q_HEADS * HEAD_DIM
    qblk = col0 // qw
    kblk = (col0 + qw) // V7X_LANES
    vblk = kblk + 1
    return pl.pallas_call(
        _swa_kernel,
        grid=(b, s // w),
        in_specs=[
            pl.BlockSpec(memory_space=pltpu.SMEM),
            pl.BlockSpec((1, w, qw), lambda bi, i: (bi, i, qblk)),
            pl.BlockSpec((1, w, V7X_LANES), lambda bi, i: (bi, jnp.maximum(i - 1, 0), kblk)),
            pl.BlockSpec((1, w, V7X_LANES), lambda bi, i: (bi, i, kblk)),
            pl.BlockSpec((1, w, V7X_LANES), lambda bi, i: (bi, jnp.maximum(i - 1, 0), vblk)),
            pl.BlockSpec((1, w, V7X_LANES), lambda bi, i: (bi, i, vblk)),
        ],
        out_specs=pl.BlockSpec((1, w, qw), lambda bi, i: (bi, i, 0)),
        out_shape=jax.ShapeDtypeStruct((b, s, qw), BF16),
        compiler_params=_params("parallel", "parallel"),
        name="swa",
    )(sinks.astype(F32), proj, proj, proj, proj, proj)


def _split3_bf16(x):
    parts, rest = [], np.float32(x)
    for _ in range(3):
        hi = (np.asarray(rest, np.float32).view(np.uint32) & np.uint32(0xFFFF0000)).view(np.float32)
        parts.append(float(hi))
        rest = np.float32(rest - hi)
    return parts


def _moba_kernel(slopes_ref, q_ref, k_ref, v_ref, o_ref, kmean_ref, rhs_ref, va_ref, lhs_ref, m_ref, acc_ref,
                 sa_ref, sb_ref, *, n_blocks, grp, qb):
    pair = pl.program_id(1)
    ti = pl.program_id(2)
    blk = MOBA_BLOCK
    lanes = V7X_LANES
    half = lanes // 2
    qt = qb * blk
    i0 = ti * qb
    n_sel = min(MOBA_TOPK, n_blocks)

    @pl.when(ti == 0)
    def _():
        kmean_ref[...] = jnp.zeros_like(kmean_ref)
        lane_b = lax.broadcasted_iota(jnp.int32, (blk, lanes), 1)
        low_b = lane_b < HEAD_DIM
        pos_in_blk = lax.broadcasted_iota(jnp.int32, (blk, lanes), 0).astype(F32)
        in_blk_lanes = (lane_b >= half) & (lane_b < half + 3)
        in_pos_lanes = (lane_b >= half + 3) & (lane_b < half + 6)

        def fill(j, carry):
            r0 = pl.multiple_of(j * blk, blk)
            blk_start = lax.convert_element_type(j * blk, F32)
            aug = jnp.where(lane_b == j, 1.0,
                            jnp.where(in_blk_lanes, blk_start, jnp.where(in_pos_lanes, pos_in_blk, 0.0)))
            rhs_ref[pl.ds(r0, blk), :] = jnp.concatenate([k_ref[0, pl.ds(r0, blk), :], aug.astype(BF16)], axis=1)
            vb = v_ref[0, pl.ds(r0, blk), :]
            va_ref[0, pl.ds(r0, blk), :] = jnp.where(low_b, vb, jnp.ones_like(vb))
            va_ref[1, pl.ds(r0, blk), :] = jnp.where(low_b, jnp.ones_like(vb), vb)
            return carry

        lax.fori_loop(0, n_blocks, fill, 0)

    row0 = pl.multiple_of(ti * qt, qt)
    for hb in range(qb):
        k_blk = k_ref[0, pl.ds(row0 + hb * blk, blk), :]
        kmean_ref[pl.ds(i0 + hb, 1), :] = jnp.mean(k_blk.astype(F32), axis=0, keepdims=True)
    kmean_b = kmean_ref[...].astype(BF16)

    lane_id = lax.broadcasted_iota(jnp.int32, (qt, lanes), 1)
    low_half = lane_id < HEAD_DIM
    own_blk = i0 + lax.broadcasted_iota(jnp.int32, (qt, lanes), 0) // blk
    qp = q_ref[0]
    rhs_own = rhs_ref[pl.ds(row0, qt), :]
    r_id = lax.broadcasted_iota(jnp.int32, (qt, qt), 0)
    c_id = lax.broadcasted_iota(jnp.int32, (qt, qt), 1)
    causal = c_id <= r_id
    nt = (((1,), (1,)), ((), ()))
    slope_part = (lane_id - half) % 3

    qms = [jnp.where(low_half, qp, jnp.zeros_like(qp)), jnp.where(low_half, jnp.zeros_like(qp), qp)]
    blk_id = lax.broadcasted_iota(jnp.int32, (half, 2 * qt), 0)
    blk_f = blk_id.astype(F32)
    own_t = i0 + (lax.broadcasted_iota(jnp.int32, (half, 2 * qt), 1) % qt) // blk
    gate = lax.dot_general(kmean_b[:half], jnp.concatenate(qms, axis=0), nt, preferred_element_type=F32)
    gate = jnp.where(blk_id < own_t, gate, -jnp.inf)
    picked_t = jnp.zeros((half, 2 * qt), F32)
    for _ in range(n_sel):
        mx = jnp.max(gate, axis=0, keepdims=True)
        first = jnp.min(jnp.where(gate == mx, blk_f, float(half)), axis=0, keepdims=True)
        pick = (blk_f == first) & (mx > -jnp.inf)
        picked_t = jnp.where(pick, 1.0, picked_t)
        gate = jnp.where(pick, -jnp.inf, gate)
    picked2 = jnp.concatenate([picked_t, jnp.zeros_like(picked_t)], axis=0).T

    for a in range(2):
        s3 = [slopes_ref[6 * pair + 3 * a + t] for t in range(3)]
        qm = qms[a]
        sel_code = jnp.where(picked2[a * qt:(a + 1) * qt] > 0.0, 0.0, MASKED_SCORE)
        slope_lanes = jnp.where(lane_id < half + 6,
                                jnp.where(slope_part == 0, s3[0], jnp.where(slope_part == 1, s3[1], s3[2])), 0.0)
        walk_code = jnp.where(low_half, jnp.where(lane_id >= i0, MASKED_SCORE, sel_code), slope_lanes)
        lhs_ref[a] = jnp.concatenate([qm, walk_code.astype(BF16)], axis=1)
        own_code = jnp.where(low_half, jnp.where(lane_id == own_blk, 0.0, sel_code), slope_lanes)
        lhs_own = jnp.concatenate([qm, own_code.astype(BF16)], axis=1)
        s = lax.dot_general(lhs_own, rhs_own, nt, preferred_element_type=F32)
        logits = jnp.where(causal, s, -jnp.inf)
        m = jnp.max(logits, axis=-1, keepdims=True)
        p = jnp.exp2(logits - m)
        m_ref[a] = jnp.broadcast_to(m, (qt, lanes))
        acc_ref[a] = jnp.dot(p.astype(BF16), va_ref[a, pl.ds(row0, qt), :], preferred_element_type=F32)

    gk = grp * blk
    n_groups = n_blocks // grp

    def scores(g, s_ref):
        r0 = pl.multiple_of(jnp.minimum(g, n_groups - 1) * gk, gk)
        rhs = rhs_ref[pl.ds(r0, gk), :]
        for a in range(2):
            s_ref[a] = lax.dot_general(lhs_ref[a], rhs, nt, preferred_element_type=F32)

    def absorb(g, s_ref):
        r0 = pl.multiple_of(g * gk, gk)
        for a in range(2):
            s = s_ref[a]
            m_old = m_ref[a]
            m_new = jnp.maximum(m_old, jnp.max(s, axis=-1, keepdims=True))
            alpha = jnp.exp2(m_old - m_new)
            p = jnp.exp2(s - jnp.concatenate([m_new] * (gk // lanes), axis=1))
            acc_ref[a] = alpha * acc_ref[a] + jnp.dot(p.astype(BF16), va_ref[a, pl.ds(r0, gk), :],
                                                      preferred_element_type=F32)
            m_ref[a] = m_new

    n_need = (i0 + grp - 1) // grp
    scores(0, sa_ref)

    def group_pair(h, carry):
        g = 2 * h
        scores(g + 1, sb_ref)
        absorb(g, sa_ref)
        scores(g + 2, sa_ref)
        absorb(g + 1, sb_ref)
        return carry

    n_full = jnp.maximum(n_need - 1, 0) // 2
    lax.fori_loop(0, n_full, group_pair, 0)
    g_tail = 2 * n_full

    @pl.when(n_need - g_tail == 1)
    def _():
        absorb(g_tail, sa_ref)

    @pl.when(n_need - g_tail == 2)
    def _():
        scores(g_tail + 1, sb_ref)
        absorb(g_tail, sa_ref)
        absorb(g_tail + 1, sb_ref)

    o0 = acc_ref[0] / pltpu.roll(acc_ref[0], half, 1)
    o1 = acc_ref[1] / pltpu.roll(acc_ref[1], half, 1)
    o_ref[0] = jnp.where(low_half, o0, o1).astype(o_ref.dtype)


def _moba(proj, col0):
    b, s, _ = proj.shape
    blk = MOBA_BLOCK
    assert s % blk == 0 and col0 % V7X_LANES == 0
    n_blocks = s // blk
    assert n_blocks <= V7X_LANES // 2
    grp = MOBA_GROUP
    while n_blocks % grp != 0:
        grp //= 2
    assert grp >= 1
    qb = MOBA_Q_BLOCKS if n_blocks % MOBA_Q_BLOCKS == 0 else 1
    qt = qb * blk
    n_pairs = MOBA_HEADS // 2
    hw = MOBA_HEADS * HEAD_DIM
    qc, kc, vc = col0 // V7X_LANES, (col0 + hw) // V7X_LANES, (col0 + 2 * hw) // V7X_LANES
    slopes = jnp.asarray([part for h in range(MOBA_HEADS)
                          for part in _split3_bf16(_alibi_slope(SWA_Q_HEADS + h) * LOG2_E)], dtype=F32)
    whole_seq = functools.partial(pl.BlockSpec, (1, s, V7X_LANES), pipeline_mode=pl.Buffered(1))
    return pl.pallas_call(
        functools.partial(_moba_kernel, n_blocks=n_blocks, grp=grp, qb=qb),
        grid=(b, n_pairs, n_blocks // qb),
        in_specs=[
            pl.BlockSpec(memory_space=pltpu.SMEM),
            pl.BlockSpec((1, qt, V7X_LANES), lambda bi, p, ti: (bi, ti, qc + p)),
            whole_seq(lambda bi, p, ti: (bi, 0, kc + p)),
            whole_seq(lambda bi, p, ti: (bi, 0, vc + p)),
        ],
        out_specs=pl.BlockSpec((1, qt, V7X_LANES), lambda bi, p, ti: (bi, ti, p)),
        out_shape=jax.ShapeDtypeStruct((b, s, hw), BF16),
        scratch_shapes=[
            pltpu.VMEM((V7X_LANES, V7X_LANES), F32),
            pltpu.VMEM((s, 2 * V7X_LANES), BF16),
            pltpu.VMEM((2, s, V7X_LANES), BF16),
            pltpu.VMEM((2, qt, 2 * V7X_LANES), BF16),
            pltpu.VMEM((2, qt, V7X_LANES), F32),
            pltpu.VMEM((2, qt, V7X_LANES), F32),
            pltpu.VMEM((2, qt, grp * blk), F32),
            pltpu.VMEM((2, qt, grp * blk), F32),
        ],
        compiler_params=_params("arbitrary", "arbitrary", "arbitrary"),
        name="moba",
    )(slopes, proj, proj, proj)


def _block_ref_rows(b, m):
    c, n = b.shape
    if 2 * m >= 8:
        b3 = b.reshape(c // (2 * m), 2 * m, n)
        return jnp.broadcast_to(b3[:, m - 1:m, :], b3.shape).reshape(c, n)
    r = lax.broadcasted_iota(jnp.int32, (c, n), 0)
    if m == 1:
        return jnp.where(r % 2 == 1, pltpu.roll(b, 1, 0), b)
    assert m == 2
    q4 = r % 4
    up1 = pltpu.roll(b, c - 1, 0)
    dn1 = pltpu.roll(b, 1, 0)
    dn2 = pltpu.roll(b, 2, 0)
    return jnp.where(q4 == 0, up1, jnp.where(q4 == 1, b, jnp.where(q4 == 2, dn1, dn2)))


def _neg_abs(x):
    bits = lax.bitcast_convert_type(x, jnp.uint32) | jnp.uint32(0x80000000)
    return lax.bitcast_convert_type(bits, F32)


def _hgrn_layer_kernel(lbp_ref, ng_ref, x_ref, xn_ref, w_in_ref, w_out_ref, g_ref, b_ref, y_ref,
                       st_ref, lv_ref, cur_ref, nxt_ref, o_ref, *, layer, chunk):
    c = chunk
    lanes = V7X_LANES
    ci = pl.program_id(1)
    t_id = lax.broadcasted_iota(jnp.int32, (c, c), 0)
    s_id = lax.broadcasted_iota(jnp.int32, (c, c), 1)

    def project(x_blk, dst_ref):
        dst_ref[...] = jnp.dot(x_blk.astype(BF16), w_in_ref[...], preferred_element_type=F32).astype(BF16)

    @pl.when(ci == 0)
    def _():
        st_ref[...] = jnp.zeros_like(st_ref)
        x = t_id ^ s_id
        code = jnp.zeros((c, c), jnp.int32)
        bit = 1
        while bit < c:
            code = code + (x >= bit).astype(jnp.int32)
            bit *= 2
        lv_ref[...] = jnp.where(s_id <= t_id, code, -1)
        project(x_ref[0], cur_ref)

    project(xn_ref[0], nxt_ref)

    tril = jnp.where(s_id <= t_id, 1.0, 0.0).astype(BF16)
    hw = HGRN_HEADS * lanes
    for hd in range(HGRN_HEADS):
        cols = slice(hd * lanes, (hd + 1) * lanes)
        q, f, v, g = (cur_ref[:, j * hw + hd * lanes:j * hw + (hd + 1) * lanes] for j in range(4))
        o_ref[:, cols] = _hgrn_head(lbp_ref[:, cols], ng_ref[...], q, f, v, g, st_ref.at[hd], lv_ref, tril,
                                    layer=layer, c=c).astype(o_ref.dtype)
    h = jnp.dot(o_ref[...], w_out_ref[...], preferred_element_type=F32)
    y = DEEPNORM_ALPHA * x_ref[0] + h
    y_ref[0] = _layer_norm_rows(y, g_ref[...], b_ref[...])
    cur_ref[...] = nxt_ref[...]


def _hgrn_head(lbp, ng, qraw, fraw, v, graw, st_ref, lv_ref, tril, *, layer, c):
    lanes = V7X_LANES
    lbp = lbp.astype(F32)
    e = jnp.exp(lbp - jnp.max(lbp, axis=0, keepdims=True))
    sm = e / jnp.sum(e, axis=0, keepdims=True)
    lb = jnp.sum(sm[1:layer + 1], axis=0, keepdims=True) if layer >= 1 else jnp.zeros((1, lbp.shape[1]), F32)

    qraw = qraw.astype(F32)
    fraw = fraw.astype(F32)
    v = v.astype(F32)
    graw = graw.astype(F32)
    q = qraw * jax.nn.sigmoid(qraw)
    f_t = lb + (1.0 - lb) * jax.nn.sigmoid(fraw)
    g = jnp.log2(f_t)
    kk = (1.0 - lb) * jax.nn.sigmoid(-fraw)

    g_hi = _bf16_truncate(g)
    r1 = g - g_hi
    g_mid = _bf16_truncate(r1)
    g_lo = r1 - g_mid
    b = (jnp.dot(tril, g_hi.astype(BF16), preferred_element_type=F32)
         + jnp.dot(tril, g_mid.astype(BF16), preferred_element_type=F32)
         + jnp.dot(tril, g_lo.astype(BF16), preferred_element_type=F32))

    nt = (((1,), (1,)), ((), ()))
    st = st_ref[...]
    o = lax.dot_general((q * jnp.exp2(b)).astype(BF16), st.astype(BF16), nt, preferred_element_type=F32)

    prods = {0: lax.dot_general(q.astype(BF16), kk.astype(BF16), nt, preferred_element_type=F32)}
    m, code = 1, 1
    while m < c:
        dec = jnp.exp2(_neg_abs(b - _block_ref_rows(b, m)))
        qe, ke = (q * dec).astype(BF16), (kk * dec).astype(BF16)
        if 2 * m == c and m % lanes == 0:
            prods[code] = lax.dot_general(qe[m:], ke[:m], nt, preferred_element_type=F32)
        else:
            prods[code] = lax.dot_general(qe, ke, nt, preferred_element_type=F32)
        m, code = m * 2, code + 1
    top = code - 1 if (c // 2) % lanes == 0 else None
    bands = []
    for r0 in range(0, c, 8):
        d0 = (r0 // lanes) * lanes
        lv = lv_ref[r0:r0 + 8, d0:d0 + lanes]
        tile = jnp.zeros((8, lanes), F32)
        for cd, p in prods.items():
            if cd == top:
                continue
            mm = 1 << (cd - 1) if cd else 0
            if cd and (r0 // mm) % 2 == 0 and mm >= 8:
                continue
            tile = jnp.where(lv == cd, p[r0:r0 + 8, d0:d0 + lanes], tile)
        cols = []
        for c0 in range(0, c, lanes):
            if c0 == d0:
                cols.append(tile)
            elif top is not None and c0 < d0:
                cols.append(prods[top][r0 - c // 2:r0 - c // 2 + 8, c0:c0 + lanes])
            else:
                cols.append(jnp.zeros((8, lanes), F32))
        bands.append(jnp.concatenate(cols, axis=1))
    a = jnp.concatenate(bands, axis=0)
    o = o + jnp.dot(a.astype(BF16), v.astype(BF16), preferred_element_type=F32)

    b_last = b[c - 1:c, :]
    k_end = (kk * jnp.exp2(b_last - b)).astype(BF16)
    st_ref[...] = st * jnp.exp2(b_last) + jnp.dot(v.T.astype(BF16), k_end, preferred_element_type=F32)

    rms = o * lax.rsqrt(jnp.mean(o * o, axis=-1, keepdims=True) + RMS_EPS) * ng.astype(F32)
    return rms * (graw * jax.nn.sigmoid(graw))


def _swa_head_order():
    order = []
    for p in range(SWA_GROUP):
        for a in range(SWA_KV_HEADS):
            h = a * SWA_GROUP + p
            order.extend(range(h * HEAD_DIM, (h + 1) * HEAD_DIM))
    return jnp.asarray(order, dtype=jnp.int32)


def _attention_mixer(x, batch, w_in, sinks, w_out):
    t, d = x.shape
    s = t // batch
    perm = _swa_head_order()
    qa_w = SWA_Q_HEADS * HEAD_DIM
    kv_w = 2 * SWA_KV_HEADS * HEAD_DIM
    qb_w = MOBA_HEADS * HEAD_DIM
    w_in_p = jnp.concatenate([w_in[:, :qa_w][:, perm], w_in[:, qa_w:qa_w + kv_w],
                              w_in[:, qa_w + kv_w:qa_w + kv_w + qb_w] * (ATTN_SCALE * LOG2_E),
                              w_in[:, qa_w + kv_w + qb_w:]], axis=1)
    proj = _proj(x, w_in_p, BF16).reshape(batch, s, -1)
    oa = _swa(proj, sinks, 0).reshape(t, qa_w)
    ob = _moba(proj, qa_w + kv_w).reshape(t, qb_w)
    return oa, ob, w_out[:qa_w][perm], w_out[qa_w:]


def _hgrn_layer(x, batch, w_in, norm_g, w_out, lower_bounds, layer, g, b):
    t, d = x.shape
    s = t // batch
    c = min(HGRN_CHUNK, s)
    assert s % c == 0 and c % 8 == 0 and (c & (c - 1)) == 0
    assert HGRN_DK == HGRN_DV == V7X_LANES
    nc = s // c
    hw = HGRN_HEADS * HGRN_DK
    depth = lower_bounds.shape[0]
    y = pl.pallas_call(
        functools.partial(_hgrn_layer_kernel, layer=layer, chunk=c),
        grid=(batch, nc),
        in_specs=[
            pl.BlockSpec((depth, hw), lambda bi, ci: (0, 0)),
            pl.BlockSpec((1, HGRN_DV), lambda bi, ci: (0, 0)),
            pl.BlockSpec((1, c, d), lambda bi, ci: (bi, ci, 0)),
            pl.BlockSpec((1, c, d), lambda bi, ci: (bi, jnp.minimum(ci + 1, nc - 1), 0)),
            _const_spec((d, 4 * hw)),
            _const_spec((hw, d)),
            _const_spec((1, d)),
            _const_spec((1, d)),
        ],
        out_specs=pl.BlockSpec((1, c, d), lambda bi, ci: (bi, ci, 0)),
        out_shape=jax.ShapeDtypeStruct((batch, s, d), F32),
        scratch_shapes=[
            pltpu.VMEM((HGRN_HEADS, HGRN_DV, HGRN_DK), F32),
            pltpu.VMEM((c, c), jnp.int32),
            pltpu.VMEM((c, 4 * hw), BF16),
            pltpu.VMEM((c, 4 * hw), BF16),
            pltpu.VMEM((c, hw), BF16),
        ],
        compiler_params=_params("arbitrary", "arbitrary"),
        name="hgrn_layer",
    )(lower_bounds, norm_g.reshape(1, HGRN_DV), x.reshape(batch, s, d), x.reshape(batch, s, d),
      w_in.astype(BF16), w_out.astype(BF16), g.reshape(1, d), b.reshape(1, d))
    return y.reshape(t, d)


def kernel(x, ffn_w1, ffn_w3, ffn_w2, ln_g, ln_b, attn_w_in, attn_sinks, attn_w_out,
           hgrn_w_in, hgrn_norm_g, hgrn_w_out, hgrn_lower_bounds):
    batch, seq, d = x.shape
    xf = x.reshape(batch * seq, d)
    depth = ffn_w1.shape[0]
    for l in range(depth):
        j = l // 2
        xf = _ffn_ln(xf, ffn_w1[l, 0], ffn_w3[l, 0], ffn_w2[l, 0], ln_g[l, 0], ln_b[l, 0])
        if l % 2 == 0:
            mix = _attention_mixer(xf, batch, attn_w_in[j], attn_sinks[j], attn_w_out[j]) + (ln_g[l, 1], ln_b[l, 1])
        else:
            xf = _hgrn_layer(xf, batch, hgrn_w_in[j], hgrn_norm_g[j], hgrn_w_out[j], hgrn_lower_bounds, l,
                             ln_g[l, 1], ln_b[l, 1])
            mix = None
        xf = _ffn_ln(xf, ffn_w1[l, 1], ffn_w3[l, 1], ffn_w2[l, 1], ln_g[l, 2], ln_b[l, 2], mix=mix)
    return xf.reshape(batch, seq, d)
```
